```python
import jax, jax.numpy as jnp
from jax import lax
import numpy as np

D_MODEL = 1024
BATCH = 32
SEQ = 2048
DEPTH = 2

D_MIX = 2 * D_MODEL
D_SSD = D_MIX // 2
SSD_HEAD_DIM = 64
SSD_HEADS = D_SSD // SSD_HEAD_DIM
SSD_GROUPS = 2
SSD_STATE = 128
SSD_CONV = 4
CHUNK = 128
MLA_HEADS = 8
QK_NOPE = 128
QK_ROPE = 64
V_DIM = 128
D_ATT = MLA_HEADS * V_DIM
Q_RANK = 384
KV_RANK = 256
ROPE_BASE = 10000.0
Q_BLOCK = 128
D_FF = 2816
FF_CONV = 3
EPS = 1e-6

D_XBC = D_SSD + 2 * SSD_GROUPS * SSD_STATE
IN_SIZES = [D_SSD, D_XBC, SSD_HEADS, Q_RANK, KV_RANK, QK_ROPE]
IN_SPLITS = np.cumsum(IN_SIZES)[:-1].tolist()
D_IN = int(sum(IN_SIZES))

kernel_name = "hybrid_ssd_mla_convglu_adaln"


def rms_norm(x, g):
    xf = x.astype(jnp.float32)
    y = xf * lax.rsqrt(jnp.mean(xf * xf, axis=-1, keepdims=True) + EPS)
    return y.astype(x.dtype) * g


def modulate(h, shift, scale):
    return h * (1 + scale[:, None, :]) + shift[:, None, :]


def causal_depthwise_conv(u, w, b):
    k = w.shape[0]
    out = lax.conv_general_dilated(
        u, w[:, None, :].astype(u.dtype), window_strides=(1,), padding=[(k - 1, 0)],
        dimension_numbers=('NWC', 'WIO', 'NWC'), feature_group_count=u.shape[-1])
    return out + b


def apply_rope(t, cos, sin):
    t1, t2 = jnp.split(t, 2, axis=-1)
    return jnp.concatenate([t1 * cos - t2 * sin, t2 * cos + t1 * sin], axis=-1)


def ssd_chunked_scan(xs, dt, a_head, bm, cm):
    bsz, s, h, p = xs.shape
    nc = s // CHUNK
    k = h // SSD_GROUPS
    x_dt = (xs * dt[..., None]).reshape(bsz, nc, CHUNK, SSD_GROUPS, k, p)
    a = (dt * a_head).reshape(bsz, nc, CHUNK, SSD_GROUPS, k).transpose(0, 1, 3, 4, 2)
    bm = bm.reshape(bsz, nc, CHUNK, SSD_GROUPS, SSD_STATE)
    cm = cm.reshape(bsz, nc, CHUNK, SSD_GROUPS, SSD_STATE)
    a_cs = jnp.cumsum(a, axis=-1)
    causal = jnp.asarray(np.tril(np.ones((CHUNK, CHUNK), dtype=bool)))
    seg = a_cs[..., :, None] - a_cs[..., None, :]
    decay_in = jnp.exp(jnp.where(causal, seg, -jnp.inf))
    cb = jnp.einsum('bclgn,bcsgn->bcgls', cm, bm)
    y_diag = jnp.einsum('bcgkls,bcsgkp->bclgkp', cb[:, :, :, None] * decay_in, x_dt)
    decay_to_end = jnp.exp(a_cs[..., -1:] - a_cs)
    chunk_states = jnp.einsum('bclgn,bcgkl,bclgkp->bcgkpn', bm, decay_to_end, x_dt)
    chunk_decay = jnp.exp(a_cs[..., -1])

    def step(state, inp):
        st, dec = inp
        return state * dec[..., None, None] + st, state

    init = jnp.zeros((bsz, SSD_GROUPS, k, p, SSD_STATE), chunk_states.dtype)
    _, prev = lax.scan(step, init, (jnp.moveaxis(chunk_states, 1, 0), jnp.moveaxis(chunk_decay, 1, 0)))
    prev = jnp.moveaxis(prev, 0, 1)
    y_off = jnp.einsum('bclgn,bcgkpn,bcgkl->bclgkp', cm, prev, jnp.exp(a_cs))
    return (y_diag + y_off).reshape(bsz, s, h, p)


def ssd_mixer(z, xbc, dt_raw, conv_w, conv_b, dt_bias, a_log, d_skip, ssd_norm):
    bsz, s, _ = z.shape
    xbc = jax.nn.silu(causal_depthwise_conv(xbc, conv_w, conv_b))
    xs, bm, cm = jnp.split(xbc, [D_SSD, D_SSD + SSD_GROUPS * SSD_STATE], axis=-1)
    xs = xs.reshape(bsz, s, SSD_HEADS, SSD_HEAD_DIM)
    bm = bm.reshape(bsz, s, SSD_GROUPS, SSD_STATE)
    cm = cm.reshape(bsz, s, SSD_GROUPS, SSD_STATE)
    dt = jax.nn.softplus(dt_raw + dt_bias)
    a_head = -jnp.exp(a_log)
    y = ssd_chunked_scan(xs, dt, a_head, bm, cm) + xs * d_skip[:, None]
    y = y.reshape(bsz, s, D_SSD) * jax.nn.silu(z)
    yg = y.reshape(bsz, s, SSD_GROUPS, D_SSD // SSD_GROUPS).astype(jnp.float32)
    yg = yg * lax.rsqrt(jnp.mean(yg * yg, axis=-1, keepdims=True) + EPS)
    return yg.reshape(bsz, s, D_SSD).astype(y.dtype) * ssd_norm


def mla_mixer(cq, ckv, k_rope_raw, cos, sin, q_norm, w_uq, kv_norm, w_ukv, attn_norm):
    bsz, s, _ = cq.shape
    q = (rms_norm(cq, q_norm) @ w_uq).reshape(bsz, s, MLA_HEADS, QK_NOPE + QK_ROPE)
    q_nope, q_rope = jnp.split(q, [QK_NOPE], axis=-1)
    q_rope = apply_rope(q_rope, cos[:, :, None, :], sin[:, :, None, :])
    kv = (rms_norm(ckv, kv_norm) @ w_ukv).reshape(bsz, s, MLA_HEADS, QK_NOPE + V_DIM)
    k_nope, v = jnp.split(kv, [QK_NOPE], axis=-1)
    k_rope = apply_rope(k_rope_raw, cos, sin)
    scale = (QK_NOPE + QK_ROPE) ** -0.5
    outs = []
    for i in range(s // Q_BLOCK):
        q0, q1 = i * Q_BLOCK, (i + 1) * Q_BLOCK
        sc = (jnp.einsum('bqhd,bkhd->bhqk', q_nope[:, q0:q1], k_nope[:, :q1])
              + jnp.einsum('bqhr,bkr->bhqk', q_rope[:, q0:q1], k_rope[:, :q1]))
        sc = sc.astype(jnp.float32) * scale
        mask = jnp.asarray(np.arange(q0, q1)[:, None] >= np.arange(q1)[None, :])
        probs = jax.nn.softmax(jnp.where(mask, sc, -jnp.inf), axis=-1).astype(v.dtype)
        outs.append(jnp.einsum('bhqk,bkhd->bqhd', probs, v[:, :q1]))
    o = jnp.concatenate(outs, axis=1).reshape(bsz, s, D_ATT)
    return rms_norm(o, attn_norm)


def conv_glu_ffn(h, w_up, conv_w, conv_b, w_down):
    u = causal_depthwise_conv(h @ w_up, conv_w, conv_b)
    gate, val = jnp.split(u, 2, axis=-1)
    return (jax.nn.silu(gate) * val) @ w_down


def setup_inputs(seed: int = 0) -> dict:
    key = jax.random.key(seed)
    ks = jax.random.split(key, 32)

    def nrm(k, shape, scale):
        return jax.random.normal(k, shape, jnp.float32) * scale

    def gain(k, shape):
        return 1.0 + nrm(k, shape, 0.02)

    L = DEPTH
    x = nrm(ks[0], (BATCH, SEQ, D_MODEL), 1.0)
    c = nrm(ks[1], (BATCH, D_MODEL), 1.0)
    offsets = jax.random.randint(ks[2], (BATCH, 1), 0, 4096, dtype=jnp.int32)
    positions = (offsets + jnp.arange(SEQ, dtype=jnp.int32)[None, :]).astype(jnp.int32)
    dt0 = jnp.exp(jax.random.uniform(ks[9], (L, SSD_HEADS), jnp.float32, np.log(1e-3), np.log(1e-1)))
    dt_bias = dt0 + jnp.log(-jnp.expm1(-dt0))
    a_log = jnp.log(jax.random.uniform(ks[10], (L, SSD_HEADS), jnp.float32, 1.0, 16.0))
    return {
        'x': x, 'c': c, 'positions': positions,
        'w_ada': nrm(ks[3], (L, D_MODEL, 6 * D_MODEL), D_MODEL ** -0.5),
        'b_ada': nrm(ks[4], (L, 6 * D_MODEL), 0.02),
        'norm_mix': gain(ks[5], (L, D_MODEL)),
        'w_in': nrm(ks[6], (L, D_MODEL, D_IN), D_MODEL ** -0.5),
        'conv_w': nrm(ks[7], (L, SSD_CONV, D_XBC), SSD_CONV ** -0.5),
        'conv_b': nrm(ks[8], (L, D_XBC), 0.02),
        'dt_bias': dt_bias,
        'a_log': a_log,
        'd_skip': gain(ks[11], (L, SSD_HEADS)),
        'ssd_norm': gain(ks[12], (L, D_SSD)),
        'q_norm': gain(ks[13], (L, Q_RANK)),
        'w_uq': nrm(ks[14], (L, Q_RANK, MLA_HEADS * (QK_NOPE + QK_ROPE)), Q_RANK ** -0.5),
        'kv_norm': gain(ks[15], (L, KV_RANK)),
        'w_ukv': nrm(ks[16], (L, KV_RANK, MLA_HEADS * (QK_NOPE + V_DIM)), KV_RANK ** -0.5),
        'attn_norm': gain(ks[17], (L, D_ATT)),
        'w_out': nrm(ks[18], (L, D_MIX, D_MODEL), D_MIX ** -0.5),
        'norm_mlp': gain(ks[19], (L, D_MODEL)),
        'w_up': nrm(ks[20], (L, D_MODEL, 2 * D_FF), D_MODEL ** -0.5),
        'conv_ff_w': nrm(ks[21], (L, FF_CONV, 2 * D_FF), FF_CONV ** -0.5),
        'conv_ff_b': nrm(ks[22], (L, 2 * D_FF), 0.02),
        'w_down': nrm(ks[23], (L, D_FF, D_MODEL), D_FF ** -0.5),
        'final_norm': gain(ks[24], (D_MODEL,)),
    }


def reference(x, c, positions, w_ada, b_ada, norm_mix, w_in, conv_w, conv_b, dt_bias, a_log,
              d_skip, ssd_norm, q_norm, w_uq, kv_norm, w_ukv, attn_norm, w_out, norm_mlp,
              w_up, conv_ff_w, conv_ff_b, w_down, final_norm):
    inv_freq = jnp.asarray(1.0 / (ROPE_BASE ** (np.arange(0, QK_ROPE, 2, dtype=np.float32) / QK_ROPE)))
    angles = positions.astype(jnp.float32)[..., None] * inv_freq
    cos = jnp.cos(angles).astype(x.dtype)
    sin = jnp.sin(angles).astype(x.dtype)
    c_act = jax.nn.silu(c)
    for l in range(DEPTH):
        mod = c_act @ w_ada[l] + b_ada[l]
        sh1, sc1, g1, sh2, sc2, g2 = jnp.split(mod, 6, axis=-1)
        h = modulate(rms_norm(x, norm_mix[l]), sh1, sc1)
        z, xbc, dt_raw, cq, ckv, kr = jnp.split(h @ w_in[l], IN_SPLITS, axis=-1)
        y_ssd = ssd_mixer(z, xbc, dt_raw, conv_w[l], conv_b[l], dt_bias[l], a_log[l], d_skip[l], ssd_norm[l])
        y_att = mla_mixer(cq, ckv, kr, cos, sin, q_norm[l], w_uq[l], kv_norm[l], w_ukv[l], attn_norm[l])
        y = jnp.concatenate([y_ssd, y_att], axis=-1) @ w_out[l]
        x = x + g1[:, None, :] * y
        h = modulate(rms_norm(x, norm_mlp[l]), sh2, sc2)
        x = x + g2[:, None, :] * conv_glu_ffn(h, w_up[l], conv_ff_w[l], conv_ff_b[l], w_down[l])
    return rms_norm(x, final_norm)
```

```python
import functools

import jax
import jax.numpy as jnp
import numpy as np
from jax import lax
from jax.experimental import pallas as pl
from jax.experimental.pallas import tpu as pltpu

F32 = jnp.float32
BF16 = jnp.bfloat16

LANES = 128
D_MODEL = 1024
D_SSD = 1024
SSD_HEADS = 16
SSD_HEAD_DIM = 64
SSD_GROUPS = 2
SSD_STATE = 128
SSD_CONV = 4
CHUNK = 128
D_XBC = D_SSD + 2 * SSD_GROUPS * SSD_STATE
GROUP_W = D_SSD // SSD_GROUPS
MLA_HEADS = 8
QK_NOPE = 128
QK_ROPE = 64
QK_PAD = 256
V_DIM = 128
D_ATT = MLA_HEADS * V_DIM
Q_RANK = 384
KV_RANK = 256
ROPE_BASE = 10000.0
D_FF = 2816
FF_CONV = 3
FF_CHUNK = 256
EPS = 1e-6
HALO = 8

OFF_Z = 0
OFF_XBC = OFF_Z + D_SSD
OFF_CQ = OFF_XBC + D_XBC
OFF_CKV = OFF_CQ + Q_RANK
OFF_MISC = OFF_CKV + KV_RANK
D_IN_PAD = OFF_MISC + LANES

VMEM_LIMIT = 56 * 1024 * 1024

TM_IN = 256
T_SSD = 512
TQ = 256
TM_FFN = 256


def _dot(a, b):
    return jnp.dot(a, b, preferred_element_type=F32)


def _dot_nt(a, b):
    return lax.dot_general(a, b, (((1,), (1,)), ((), ())), preferred_element_type=F32)


def _sigmoid(v):
    return 1.0 / (1.0 + jnp.exp(-v))


def _rms(v):
    return v * lax.rsqrt(jnp.mean(v * v, axis=-1, keepdims=True) + EPS)


def _const_spec(shape):
    nd = len(shape)
    return pl.BlockSpec(shape, lambda *_: (0,) * nd, pipeline_mode=pl.Buffered(1))


def _params(n_axes):
    return pltpu.CompilerParams(dimension_semantics=("arbitrary",) * n_axes,
                                vmem_limit_bytes=VMEM_LIMIT)


def _mod_kernel(c_ref, w_ref, b_ref, o_ref):
    c = c_ref[...]
    c_act = (c * _sigmoid(c)).astype(BF16)
    o_ref[0] = _dot(c_act, w_ref[0].astype(BF16)) + b_ref[0]


def _modulation(c, w_ada, b_ada):
    depth, _, n_out = w_ada.shape
    bsz = c.shape[0]
    nblk = n_out // D_MODEL
    return pl.pallas_call(
        _mod_kernel,
        grid=(depth, nblk),
        in_specs=[pl.BlockSpec((bsz, D_MODEL), lambda l, j: (0, 0)),
                  pl.BlockSpec((1, D_MODEL, D_MODEL), lambda l, j: (l, 0, j)),
                  pl.BlockSpec((1, 1, D_MODEL), lambda l, j: (l, 0, j))],
        out_specs=pl.BlockSpec((1, bsz, D_MODEL), lambda l, j: (l, 0, j)),
        out_shape=jax.ShapeDtypeStruct((depth, bsz, n_out), F32),
        compiler_params=_params(2),
        name="adaln_mod",
    )(c, w_ada, b_ada.reshape(depth, 1, n_out))


def _rope_kernel(pos_ref, freq_ref, sign_ref, cos_ref, sin_ref):
    ang = pos_ref[0] * freq_ref[...]
    cos_ref[0] = jnp.cos(ang)
    sin_ref[0] = jnp.sin(ang) * sign_ref[...]


def _rope_tables(positions):
    bsz, seq = positions.shape
    half = QK_ROPE // 2
    inv_freq = 1.0 / (ROPE_BASE ** (np.arange(0, QK_ROPE, 2, dtype=np.float32) / QK_ROPE))
    freq = jnp.asarray(np.tile(inv_freq, LANES // half)[None, :], F32)
    sign = jnp.asarray(np.tile(np.concatenate([-np.ones(half), np.ones(half)]), LANES // QK_ROPE)[None, :], F32)
    pos = jnp.broadcast_to(positions.astype(F32)[..., None], (bsz, seq, LANES))
    tile = min(seq, 1024)
    spec = pl.BlockSpec((1, tile, LANES), lambda b, s: (b, s, 0))
    row = pl.BlockSpec((1, LANES), lambda b, s: (0, 0))
    return pl.pallas_call(
        _rope_kernel,
        grid=(bsz, seq // tile),
        in_specs=[spec, row, row],
        out_specs=[spec, spec],
        out_shape=[jax.ShapeDtypeStruct((bsz, seq, LANES), F32)] * 2,
        compiler_params=_params(2),
        name="rope_tables",
    )(pos, freq, sign)


def _in_kernel(x_ref, mod_ref, gmix_ref, win_ref, qn_ref, wuq_ref, kvn_ref, wukv_ref, cos_ref, sin_ref,
               z_ref, xbc_ref, misc_ref, q_ref, k_ref, v_ref):
    tm = x_ref.shape[1]
    shift = mod_ref[:, 0:D_MODEL]
    scale = mod_ref[:, D_MODEL:2 * D_MODEL]
    h = (_rms(x_ref[0]) * gmix_ref[...]) * (1.0 + scale) + shift
    hb = h.astype(BF16)

    z_ref[0] = _dot(hb, win_ref[:, OFF_Z:OFF_XBC]).astype(BF16)
    xbc_ref[0] = _dot(hb, win_ref[:, OFF_XBC:OFF_CQ]).astype(BF16)
    cq = _dot(hb, win_ref[:, OFF_CQ:OFF_CKV])
    ckv = _dot(hb, win_ref[:, OFF_CKV:OFF_MISC])
    misc = _dot(hb, win_ref[:, OFF_MISC:D_IN_PAD])
    misc_ref[0] = misc

    cos = cos_ref[0]
    sin = sin_ref[0]
    lane = lax.broadcasted_iota(jnp.int32, (tm, LANES), 1)
    first_half = (lane & (QK_ROPE // 2)) == 0
    low_half = lane < QK_ROPE
    zero = jnp.zeros((tm, LANES), F32)

    def rope(r):
        swapped = jnp.where(first_half, pltpu.roll(r, LANES - QK_ROPE // 2, 1), pltpu.roll(r, QK_ROPE // 2, 1))
        return r * cos + swapped * sin

    q = _dot((_rms(cq) * qn_ref[...]).astype(BF16), wuq_ref[...])
    kv = _dot((_rms(ckv) * kvn_ref[...]).astype(BF16), wukv_ref[...])

    k_rope = jnp.where(low_half, pltpu.roll(rope(misc), QK_ROPE, 1), zero).astype(BF16)
    nope_w = MLA_HEADS * QK_NOPE
    for pair in range(MLA_HEADS // 2):
        rq = rope(q[:, nope_w + pair * LANES:nope_w + (pair + 1) * LANES])
        q_ref[0, 2 * pair, :, QK_NOPE:QK_PAD] = jnp.where(low_half, rq, zero).astype(BF16)
        q_ref[0, 2 * pair + 1, :, QK_NOPE:QK_PAD] = jnp.where(low_half, pltpu.roll(rq, QK_ROPE, 1), zero).astype(BF16)
    for hd in range(MLA_HEADS):
        q_ref[0, hd, :, 0:QK_NOPE] = q[:, hd * QK_NOPE:(hd + 1) * QK_NOPE].astype(BF16)
        k_ref[0, hd, :, 0:QK_NOPE] = kv[:, hd * QK_NOPE:(hd + 1) * QK_NOPE].astype(BF16)
        k_ref[0, hd, :, QK_NOPE:QK_PAD] = k_rope
        v_ref[0, hd] = kv[:, nope_w + hd * V_DIM:nope_w + (hd + 1) * V_DIM].astype(BF16)


def _in_proj(x, mod_l, gmix, win, qn, wuq, kvn, wukv, cos, sin):
    bsz, seq, _ = x.shape
    tm = min(TM_IN, seq)
    tok = lambda w: pl.BlockSpec((1, tm, w), lambda b, s: (b, s, 0))
    head = lambda w: pl.BlockSpec((1, MLA_HEADS, tm, w), lambda b, s: (b, 0, s, 0))
    return pl.pallas_call(
        _in_kernel,
        grid=(bsz, seq // tm),
        in_specs=[tok(D_MODEL),
                  pl.BlockSpec((None, 1, 6 * D_MODEL), lambda b, s: (b, 0, 0)),
                  _const_spec((1, D_MODEL)),
                  _const_spec((D_MODEL, D_IN_PAD)),
                  _const_spec((1, Q_RANK)),
                  _const_spec((Q_RANK, MLA_HEADS * (QK_NOPE + QK_ROPE))),
                  _const_spec((1, KV_RANK)),
                  _const_spec((KV_RANK, MLA_HEADS * (QK_NOPE + V_DIM))),
                  tok(LANES), tok(LANES)],
        out_specs=[tok(D_SSD), tok(D_XBC), tok(LANES), head(QK_PAD), head(QK_PAD), head(V_DIM)],
        out_shape=[jax.ShapeDtypeStruct((bsz, seq, D_SSD), BF16),
                   jax.ShapeDtypeStruct((bsz, seq, D_XBC), BF16),
                   jax.ShapeDtypeStruct((bsz, seq, LANES), F32),
                   jax.ShapeDtypeStruct((bsz, MLA_HEADS, seq, QK_PAD), BF16),
                   jax.ShapeDtypeStruct((bsz, MLA_HEADS, seq, QK_PAD), BF16),
                   jax.ShapeDtypeStruct((bsz, MLA_HEADS, seq, V_DIM), BF16)],
        compiler_params=_params(2),
        name="in_proj",
    )(x, mod_l, gmix, win, qn, wuq, kvn, wukv, cos, sin)


def _ssd_kernel(z_ref, xbc_ref, misc_ref, cw_ref, cb_ref, dtb_ref, alog_ref, dskip_ref, norm_ref, expand_ref,
                y_ref, ext_scr, act_scr, dt_scr, state_scr):
    t = z_ref.shape[1]
    n_chunks = t // CHUNK

    @pl.when(pl.program_id(1) == 0)
    def _():
        ext_scr[0:HALO, :] = jnp.zeros((HALO, D_XBC), F32)
        state_scr[...] = jnp.zeros(state_scr.shape, F32)

    ext_scr[HALO:HALO + t, :] = xbc_ref[0].astype(F32)
    conv = cb_ref[...] + cw_ref[0:1, :] * ext_scr[HALO - 3:HALO - 3 + t, :]
    for tap in range(1, SSD_CONV):
        conv = conv + cw_ref[tap:tap + 1, :] * ext_scr[HALO - 3 + tap:HALO - 3 + tap + t, :]
    ext_scr[0:HALO, :] = ext_scr[t:t + HALO, :]
    act_scr[...] = conv * _sigmoid(conv)

    lane_row = lax.broadcasted_iota(jnp.int32, (1, LANES), 1)
    a_head = jnp.where(lane_row < SSD_HEADS, -jnp.exp(alog_ref[...]), 0.0)
    dt_in = misc_ref[0] + dtb_ref[...]
    dt_scr[...] = jnp.maximum(dt_in, 0.0) + jnp.log1p(jnp.exp(-jnp.abs(dt_in)))

    row_i = lax.broadcasted_iota(jnp.int32, (CHUNK, CHUNK), 0)
    col_i = lax.broadcasted_iota(jnp.int32, (CHUNK, CHUNK), 1)
    causal = row_i >= col_i
    tril = jnp.where(causal, 1.0, 0.0).astype(BF16)
    lane_c = lax.broadcasted_iota(jnp.int32, (CHUNK, LANES), 1)
    left = lane_c < SSD_HEAD_DIM

    def chunk_body(c, carry):
        r0 = pl.multiple_of(c * CHUNK, CHUNK)
        rows = pl.ds(r0, CHUNK)
        dt = dt_scr[rows, :]
        a = dt * a_head
        a_hi = a.astype(BF16)
        a_lo = (a - a_hi.astype(F32)).astype(BF16)
        a_cs = _dot(tril, a_hi) + _dot(tril, a_lo)
        a_cs_t = a_cs.T
        dt_t = dt.T
        last = a_cs[CHUNK - 1:CHUNK, :]
        decay_end_dt = jnp.exp(last - a_cs) * dt
        exp_a = jnp.exp(a_cs)
        w_exp = _dot(decay_end_dt.astype(BF16), expand_ref[...])
        ea_exp = _dot(exp_a.astype(BF16), expand_ref[...])

        xs = act_scr[rows, 0:D_SSD]
        xs_b = xs.astype(BF16)
        x_w = (xs * w_exp).astype(BF16)

        heads_per_group = SSD_HEADS // SSD_GROUPS
        outs = []
        for g in range(SSD_GROUPS):
            bm = act_scr[rows, D_SSD + g * SSD_STATE:D_SSD + (g + 1) * SSD_STATE]
            cm = act_scr[rows, D_SSD + (SSD_GROUPS + g) * SSD_STATE:D_SSD + (SSD_GROUPS + g + 1) * SSD_STATE]
            cm_b = cm.astype(BF16)
            cb = _dot_nt(cm_b, bm.astype(BF16))
            y_pairs = []
            for pair in range(heads_per_group // 2):
                k0 = g * heads_per_group + 2 * pair
                ms = []
                for k in (k0, k0 + 1):
                    seg = a_cs[:, k:k + 1] - a_cs_t[k:k + 1, :]
                    decay = jnp.exp(jnp.where(causal, seg, -jnp.inf))
                    ms.append((cb * decay * dt_t[k:k + 1, :]).astype(BF16))
                xp = xs_b[:, k0 * SSD_HEAD_DIM:k0 * SSD_HEAD_DIM + LANES]
                zero = jnp.zeros_like(xp)
                block_diag = jnp.concatenate([jnp.where(left, xp, zero), jnp.where(left, zero, xp)], axis=0)
                y_pairs.append(_dot(jnp.concatenate(ms, axis=1), block_diag))
            y_diag = jnp.concatenate(y_pairs, axis=1)
            gs = slice(g * GROUP_W, (g + 1) * GROUP_W)
            prev = state_scr[g]
            y_off = _dot(cm_b, prev.astype(BF16)) * ea_exp[:, gs]
            state_scr[g] = prev * ea_exp[CHUNK - 1:CHUNK, gs] + _dot(bm.T.astype(BF16), x_w[:, gs])

            y = y_diag + y_off + xs[:, gs] * dskip_ref[:, gs]
            zc = z_ref[0, rows, gs].astype(F32)
            y = y * (zc * _sigmoid(zc))
            outs.append(_rms(y) * norm_ref[:, gs])
        y_ref[0, rows, :] = jnp.concatenate(outs, axis=1).astype(BF16)
        return carry

    lax.fori_loop(0, n_chunks, chunk_body, 0)


def _ssd_mixer(z, xbc, misc, cw, cb, dtb, alog, dskip, norm, expand):
    bsz, seq, _ = z.shape
    t = min(T_SSD, seq)
    tok = lambda w: pl.BlockSpec((1, t, w), lambda b, s: (b, s, 0))
    return pl.pallas_call(
        _ssd_kernel,
        grid=(bsz, seq // t),
        in_specs=[tok(D_SSD), tok(D_XBC), tok(LANES),
                  _const_spec((SSD_CONV, D_XBC)), _const_spec((1, D_XBC)),
                  _const_spec((1, LANES)), _const_spec((1, LANES)),
                  _const_spec((1, D_SSD)), _const_spec((1, D_SSD)),
                  _const_spec((LANES, D_SSD))],
        out_specs=tok(D_SSD),
        out_shape=jax.ShapeDtypeStruct((bsz, seq, D_SSD), BF16),
        scratch_shapes=[pltpu.VMEM((t + HALO, D_XBC), F32),
                        pltpu.VMEM((t, D_XBC), F32),
                        pltpu.VMEM((t, LANES), F32),
                        pltpu.VMEM((SSD_GROUPS, SSD_STATE, GROUP_W), F32)],
        compiler_params=_params(2),
        name="ssd_mixer",
    )(z, xbc, misc, cw, cb, dtb, alog, dskip, norm, expand)


def _attn_kernel(q_ref, k_ref, v_ref, o_ref, o_scr):
    tq = q_ref.shape[2]
    qi = pl.program_id(1)
    scale = (QK_NOPE + QK_ROPE) ** -0.5
    row_i = lax.broadcasted_iota(jnp.int32, (tq, tq), 0)
    col_i = lax.broadcasted_iota(jnp.int32, (tq, tq), 1)
    causal = row_i >= col_i

    def head_body(hd, carry):
        q = q_ref[0, hd]

        def step(start, masked, state):
            m, l, acc = state
            kv_rows = pl.ds(start, tq)
            s = _dot_nt(q, k_ref[0, hd, kv_rows, :]) * scale
            if masked:
                s = jnp.where(causal, s, -jnp.inf)
            m_new = jnp.maximum(m, jnp.max(s, axis=-1, keepdims=True))
            alpha = jnp.exp(m - m_new)
            p = jnp.exp(s - m_new)
            l = alpha * l + jnp.sum(p, axis=-1, keepdims=True)
            acc = alpha * acc + _dot(p.astype(BF16), v_ref[0, hd, kv_rows, :])
            return m_new, l, acc

        init = (jnp.full((tq, 1), -jnp.inf, F32), jnp.zeros((tq, 1), F32), jnp.zeros((tq, V_DIM), F32))
        state = lax.fori_loop(0, qi, lambda j, st: step(pl.multiple_of(j * tq, tq), False, st), init)
        _, l, acc = step(pl.multiple_of(qi * tq, tq), True, state)
        o_scr[hd] = acc / l
        return carry

    lax.fori_loop(0, MLA_HEADS, head_body, 0)
    for hd in range(MLA_HEADS):
        o_ref[0, :, hd * V_DIM:(hd + 1) * V_DIM] = o_scr[hd].astype(BF16)


def _attention(q, k, v):
    bsz, _, seq, _ = q.shape
    tq = min(TQ, seq)
    return pl.pallas_call(
        _attn_kernel,
        grid=(bsz, seq // tq),
        in_specs=[pl.BlockSpec((1, MLA_HEADS, tq, QK_PAD), lambda b, i: (b, 0, i, 0)),
                  pl.BlockSpec((1, MLA_HEADS, seq, QK_PAD), lambda b, i: (b, 0, 0, 0)),
                  pl.BlockSpec((1, MLA_HEADS, seq, V_DIM), lambda b, i: (b, 0, 0, 0))],
        out_specs=pl.BlockSpec((1, tq, D_ATT), lambda b, i: (b, i, 0)),
        out_shape=jax.ShapeDtypeStruct((bsz, seq, D_ATT), BF16),
        scratch_shapes=[pltpu.VMEM((MLA_HEADS, tq, V_DIM), F32)],
        compiler_params=_params(2),
        name="mla_attention",
    )(q, k, v)


def _out_ffn_kernel(final, y_ref, o_ref, x_ref, mod_ref, anorm_ref, wout_ref, gmlp_ref, wup_ref, cw_ref, cb_ref,
                    wdown_ref, fnorm_ref, out_ref, u_scr, act_scr):
    tm = x_ref.shape[1]

    @pl.when(pl.program_id(1) == 0)
    def _():
        u_scr[0:HALO, :] = jnp.zeros((HALO, 2 * D_FF), F32)

    gate1 = mod_ref[:, 2 * D_MODEL:3 * D_MODEL]
    shift2 = mod_ref[:, 3 * D_MODEL:4 * D_MODEL]
    scale2 = mod_ref[:, 4 * D_MODEL:5 * D_MODEL]
    gate2 = mod_ref[:, 5 * D_MODEL:6 * D_MODEL]

    o_n = (_rms(o_ref[0].astype(F32)) * anorm_ref[...]).astype(BF16)
    y = _dot(y_ref[0], wout_ref[0:D_SSD, :]) + _dot(o_n, wout_ref[D_SSD:D_SSD + D_ATT, :])
    x1 = x_ref[0] + gate1 * y
    hb = ((_rms(x1) * gmlp_ref[...]) * (1.0 + scale2) + shift2).astype(BF16)

    base = HALO - (FF_CONV - 1)
    for c in range(D_FF // FF_CHUNK):
        conv = []
        for off in (c * FF_CHUNK, D_FF + c * FF_CHUNK):
            cols = slice(off, off + FF_CHUNK)
            u_scr[HALO:HALO + tm, cols] = _dot(hb, wup_ref[:, cols])
            acc = cb_ref[:, cols] + cw_ref[0:1, cols] * u_scr[base:base + tm, cols]
            for tap in range(1, FF_CONV):
                acc = acc + cw_ref[tap:tap + 1, cols] * u_scr[base + tap:base + tap + tm, cols]
            conv.append(acc)
        act_scr[:, c * FF_CHUNK:(c + 1) * FF_CHUNK] = (conv[0] * _sigmoid(conv[0]) * conv[1]).astype(BF16)
    u_scr[0:HALO, :] = u_scr[tm:tm + HALO, :]

    x2 = x1 + gate2 * _dot(act_scr[...], wdown_ref[...])
    if final:
        x2 = _rms(x2) * fnorm_ref[...]
    out_ref[0] = x2


def _out_ffn(final, y_ssd, o, x, mod_l, anorm, wout, gmlp, wup, cw, cb, wdown, fnorm):
    bsz, seq, _ = x.shape
    tm = min(TM_FFN, seq)
    tok = lambda w: pl.BlockSpec((1, tm, w), lambda b, s: (b, s, 0))
    return pl.pallas_call(
        functools.partial(_out_ffn_kernel, final),
        grid=(bsz, seq // tm),
        in_specs=[tok(D_SSD), tok(D_ATT), tok(D_MODEL),
                  pl.BlockSpec((None, 1, 6 * D_MODEL), lambda b, s: (b, 0, 0)),
                  _const_spec((1, D_ATT)),
                  _const_spec((D_SSD + D_ATT, D_MODEL)),
                  _const_spec((1, D_MODEL)),
                  _const_spec((D_MODEL, 2 * D_FF)),
                  _const_spec((FF_CONV, 2 * D_FF)),
                  _const_spec((1, 2 * D_FF)),
                  _const_spec((D_FF, D_MODEL)),
                  _const_spec((1, D_MODEL))],
        out_specs=tok(D_MODEL),
        out_shape=jax.ShapeDtypeStruct((bsz, seq, D_MODEL), F32),
        scratch_shapes=[pltpu.VMEM((tm + HALO, 2 * D_FF), F32),
                        pltpu.VMEM((tm, D_FF), BF16)],
        compiler_params=_params(2),
        name="out_ffn_final" if final else "out_ffn",
    )(y_ssd, o, x, mod_l, anorm, wout, gmlp, wup, cw, cb, wdown, fnorm)


def _pad_lanes(v):
    return jnp.pad(v.astype(F32), (0, LANES - v.shape[0]))[None, :]


def kernel(x, c, positions, w_ada, b_ada, norm_mix, w_in, conv_w, conv_b, dt_bias, a_log, d_skip, ssd_norm,
           q_norm, w_uq, kv_norm, w_ukv, attn_norm, w_out, norm_mlp, w_up, conv_ff_w, conv_ff_b, w_down,
           final_norm):
    depth = w_in.shape[0]
    bsz = x.shape[0]
    mod = _modulation(c, w_ada, b_ada).reshape(depth, bsz, 1, 6 * D_MODEL)
    cos, sin = _rope_tables(positions)

    expand = jnp.asarray(np.kron(np.eye(LANES, SSD_HEADS, dtype=np.float32),
                                 np.ones((1, SSD_HEAD_DIM), np.float32)), BF16)

    s_z, s_xbc, s_dt = D_SSD, D_SSD + D_XBC, D_SSD + D_XBC + SSD_HEADS
    s_cq, s_ckv = s_dt + Q_RANK, s_dt + Q_RANK + KV_RANK

    for l in range(depth):
        wi = w_in[l]
        win = jnp.concatenate(
            [wi[:, :s_z], wi[:, s_z:s_xbc], wi[:, s_dt:s_cq], wi[:, s_cq:s_ckv], wi[:, s_xbc:s_dt],
             jnp.zeros((D_MODEL, LANES - SSD_HEADS - QK_ROPE), F32), wi[:, s_ckv:]], axis=1).astype(BF16)
        uq = w_uq[l].reshape(Q_RANK, MLA_HEADS, QK_NOPE + QK_ROPE)
        wuq = jnp.concatenate([uq[:, :, :QK_NOPE].reshape(Q_RANK, -1), uq[:, :, QK_NOPE:].reshape(Q_RANK, -1)],
                              axis=1).astype(BF16)
        ukv = w_ukv[l].reshape(KV_RANK, MLA_HEADS, QK_NOPE + V_DIM)
        wukv = jnp.concatenate([ukv[:, :, :QK_NOPE].reshape(KV_RANK, -1), ukv[:, :, QK_NOPE:].reshape(KV_RANK, -1)],
                               axis=1).astype(BF16)

        z, xbc, misc, q, k, v = _in_proj(x, mod[l], norm_mix[l][None, :], win, q_norm[l][None, :], wuq,
                                         kv_norm[l][None, :], wukv, cos, sin)
        y_ssd = _ssd_mixer(z, xbc, misc, conv_w[l], conv_b[l][None, :], _pad_lanes(dt_bias[l]), _pad_lanes(a_log[l]),
                           jnp.repeat(d_skip[l], SSD_HEAD_DIM)[None, :], ssd_norm[l][None, :], expand)
        o = _attention(q, k, v)
        x = _out_ffn(l == depth - 1, y_ssd, o, x, mod[l], attn_norm[l][None, :], w_out[l].astype(BF16),
                     norm_mlp[l][None, :], w_up[l].astype(BF16), conv_ff_w[l], conv_ff_b[l][None, :],
                     w_down[l].astype(BF16), final_norm[None, :])
    return x
```

```python
import functools

import jax
import jax.numpy as jnp
import numpy as np
from jax import lax
from jax.experimental import pallas as pl
from jax.experimental.pallas import tpu as pltpu

F32 = jnp.float32
BF16 = jnp.bfloat16

LANES = 128
D_MODEL = 1024
D_SSD = 1024
SSD_HEADS = 16
SSD_HEAD_DIM = 64
SSD_GROUPS = 2
SSD_STATE = 128
SSD_CONV = 4
CHUNK = 128
D_XBC = D_SSD + 2 * SSD_GROUPS * SSD_STATE
GROUP_W = D_SSD // SSD_GROUPS
MLA_HEADS = 8
QK_NOPE = 128
QK_ROPE = 64
QK_PAD = 256
V_DIM = 128
D_ATT = MLA_HEADS * V_DIM
Q_RANK = 384
KV_RANK = 256
ROPE_BASE = 10000.0
D_FF = 2816
FF_CONV = 3
FF_CHUNK = 256
EPS = 1e-6
HALO = 8

OFF_Z = 0
OFF_XBC = OFF_Z + D_SSD
OFF_CQ = OFF_XBC + D_XBC
OFF_CKV = OFF_CQ + Q_RANK
OFF_MISC = OFF_CKV + KV_RANK
D_IN_PAD = OFF_MISC + LANES

VMEM_LIMIT = 56 * 1024 * 1024

T_ATT = 256
T_SSD = 512
TM_FFN = 256

Q_PRESCALE = float((QK_NOPE + QK_ROPE) ** -0.5 * np.log2(np.e))


def _dot(a, b):
    return jnp.dot(a, b, preferred_element_type=F32)


def _dot_nt(a, b):
    return lax.dot_general(a, b, (((1,), (1,)), ((), ())), preferred_element_type=F32)


def _sigmoid(v):
    return 1.0 / (1.0 + jnp.exp(-v))


def _rms(v):
    return v * lax.rsqrt(jnp.mean(v * v, axis=-1, keepdims=True) + EPS)


def _const_spec(shape):
    nd = len(shape)
    return pl.BlockSpec(shape, lambda *_: (0,) * nd, pipeline_mode=pl.Buffered(1))


def _params(n_axes):
    return pltpu.CompilerParams(dimension_semantics=("arbitrary",) * n_axes,
                                vmem_limit_bytes=VMEM_LIMIT)


def _mod_kernel(c_ref, w_ref, b_ref, o_ref):
    c = c_ref[...]
    c_act = (c * _sigmoid(c)).astype(BF16)
    o_ref[0] = _dot(c_act, w_ref[0].astype(BF16)) + b_ref[0]


def _modulation(c, w_ada, b_ada):
    depth, _, n_out = w_ada.shape
    bsz = c.shape[0]
    nblk = n_out // D_MODEL
    return pl.pallas_call(
        _mod_kernel,
        grid=(depth, nblk),
        in_specs=[pl.BlockSpec((bsz, D_MODEL), lambda l, j: (0, 0)),
                  pl.BlockSpec((1, D_MODEL, D_MODEL), lambda l, j: (l, 0, j)),
                  pl.BlockSpec((1, 1, D_MODEL), lambda l, j: (l, 0, j))],
        out_specs=pl.BlockSpec((1, bsz, D_MODEL), lambda l, j: (l, 0, j)),
        out_shape=jax.ShapeDtypeStruct((depth, bsz, n_out), F32),
        compiler_params=_params(2),
        name="adaln_mod",
    )(c, w_ada, b_ada.reshape(depth, 1, n_out))


def _rope_kernel(pos_ref, freq_ref, sign_ref, cos_ref, sin_ref):
    ang = pos_ref[0] * freq_ref[...]
    cos_ref[0] = jnp.cos(ang)
    sin_ref[0] = jnp.sin(ang) * sign_ref[...]


def _rope_tables(positions):
    bsz, seq = positions.shape
    half = QK_ROPE // 2
    inv_freq = 1.0 / (ROPE_BASE ** (np.arange(0, QK_ROPE, 2, dtype=np.float32) / QK_ROPE))
    freq = jnp.asarray(np.tile(inv_freq, LANES // half)[None, :], F32)
    sign = jnp.asarray(np.tile(np.concatenate([-np.ones(half), np.ones(half)]), LANES // QK_ROPE)[None, :], F32)
    pos = jnp.broadcast_to(positions.astype(F32)[..., None], (bsz, seq, LANES))
    tile = min(seq, 1024)
    spec = pl.BlockSpec((1, tile, LANES), lambda b, s: (b, s, 0))
    row = pl.BlockSpec((1, LANES), lambda b, s: (0, 0))
    return pl.pallas_call(
        _rope_kernel,
        grid=(bsz, seq // tile),
        in_specs=[spec, row, row],
        out_specs=[spec, spec],
        out_shape=[jax.ShapeDtypeStruct((bsz, seq, LANES), F32)] * 2,
        compiler_params=_params(2),
        name="rope_tables",
    )(pos, freq, sign)


def _in_kernel(x_ref, mod_ref, gmix_ref, win_ref, qn_ref, wuq_ref, kvn_ref, wukv_ref, cos_ref, sin_ref,
               z_ref, xbc_ref, misc_ref, q_ref, k_ref, vt_ref):
    tm = x_ref.shape[1]
    shift = mod_ref[:, 0:D_MODEL]
    scale = mod_ref[:, D_MODEL:2 * D_MODEL]
    h = (_rms(x_ref[0]) * gmix_ref[...]) * (1.0 + scale) + shift
    hb = h.astype(BF16)

    z_ref[0] = _dot(hb, win_ref[:, OFF_Z:OFF_XBC]).astype(BF16)
    xbc_ref[0] = _dot(hb, win_ref[:, OFF_XBC:OFF_CQ]).astype(BF16)
    cq = _dot(hb, win_ref[:, OFF_CQ:OFF_CKV])
    ckv = _dot(hb, win_ref[:, OFF_CKV:OFF_MISC])
    misc = _dot(hb, win_ref[:, OFF_MISC:D_IN_PAD])
    misc_ref[0] = misc

    cos = cos_ref[0]
    sin = sin_ref[0]
    lane = lax.broadcasted_iota(jnp.int32, (tm, LANES), 1)
    first_half = (lane & (QK_ROPE // 2)) == 0
    low_half = lane < QK_ROPE
    zero = jnp.zeros((tm, LANES), F32)

    def rope(r):
        swapped = jnp.where(first_half, pltpu.roll(r, LANES - QK_ROPE // 2, 1), pltpu.roll(r, QK_ROPE // 2, 1))
        return r * cos + swapped * sin

    q = _dot((_rms(cq) * qn_ref[...]).astype(BF16), wuq_ref[...]) * Q_PRESCALE
    kv = _dot((_rms(ckv) * kvn_ref[...]).astype(BF16), wukv_ref[...])

    k_rope = jnp.where(low_half, pltpu.roll(rope(misc), QK_ROPE, 1), zero).astype(BF16)
    nope_w = MLA_HEADS * QK_NOPE
    for pair in range(MLA_HEADS // 2):
        rq = rope(q[:, nope_w + pair * LANES:nope_w + (pair + 1) * LANES])
        q_ref[0, 2 * pair, :, QK_NOPE:QK_PAD] = jnp.where(low_half, rq, zero).astype(BF16)
        q_ref[0, 2 * pair + 1, :, QK_NOPE:QK_PAD] = jnp.where(low_half, pltpu.roll(rq, QK_ROPE, 1), zero).astype(BF16)
    for hd in range(MLA_HEADS):
        q_ref[0, hd, :, 0:QK_NOPE] = q[:, hd * QK_NOPE:(hd + 1) * QK_NOPE].astype(BF16)
        k_ref[0, hd, :, 0:QK_NOPE] = kv[:, hd * QK_NOPE:(hd + 1) * QK_NOPE].astype(BF16)
        k_ref[0, hd, :, QK_NOPE:QK_PAD] = k_rope
        vt_ref[0, hd, 0] = kv[:, nope_w + hd * V_DIM:nope_w + (hd + 1) * V_DIM].T.astype(BF16)


def _in_proj(x, mod_l, gmix, win, qn, wuq, kvn, wukv, cos, sin):
    bsz, seq, _ = x.shape
    tm = min(T_ATT, seq)
    tok = lambda w: pl.BlockSpec((1, tm, w), lambda b, s: (b, s, 0))
    head = lambda w: pl.BlockSpec((1, MLA_HEADS, tm, w), lambda b, s: (b, 0, s, 0))
    return pl.pallas_call(
        _in_kernel,
        grid=(bsz, seq // tm),
        in_specs=[tok(D_MODEL),
                  pl.BlockSpec((None, 1, 6 * D_MODEL), lambda b, s: (b, 0, 0)),
                  _const_spec((1, D_MODEL)),
                  _const_spec((D_MODEL, D_IN_PAD)),
                  _const_spec((1, Q_RANK)),
                  _const_spec((Q_RANK, MLA_HEADS * (QK_NOPE + QK_ROPE))),
                  _const_spec((1, KV_RANK)),
                  _const_spec((KV_RANK, MLA_HEADS * (QK_NOPE + V_DIM))),
                  tok(LANES), tok(LANES)],
        out_specs=[tok(D_SSD), tok(D_XBC), tok(LANES), head(QK_PAD), head(QK_PAD),
                   pl.BlockSpec((1, MLA_HEADS, 1, V_DIM, tm), lambda b, s: (b, 0, s, 0, 0))],
        out_shape=[jax.ShapeDtypeStruct((bsz, seq, D_SSD), BF16),
                   jax.ShapeDtypeStruct((bsz, seq, D_XBC), BF16),
                   jax.ShapeDtypeStruct((bsz, seq, LANES), F32),
                   jax.ShapeDtypeStruct((bsz, MLA_HEADS, seq, QK_PAD), BF16),
                   jax.ShapeDtypeStruct((bsz, MLA_HEADS, seq, QK_PAD), BF16),
                   jax.ShapeDtypeStruct((bsz, MLA_HEADS, seq // tm, V_DIM, tm), BF16)],
        compiler_params=_params(2),
        name="in_proj",
    )(x, mod_l, gmix, win, qn, wuq, kvn, wukv, cos, sin)


def _ssd_kernel(z_ref, xbc_ref, misc_ref, cw_ref, cb_ref, dtb_ref, alog_ref, dskip_ref, norm_ref, expand_ref,
                y_ref, ext_scr, act_scr, dt_scr, state_scr):
    t = z_ref.shape[1]
    n_chunks = t // CHUNK

    @pl.when(pl.program_id(1) == 0)
    def _():
        ext_scr[0:HALO, :] = jnp.zeros((HALO, D_XBC), F32)
        state_scr[...] = jnp.zeros(state_scr.shape, F32)

    ext_scr[HALO:HALO + t, :] = xbc_ref[0].astype(F32)
    conv = cb_ref[...] + cw_ref[0:1, :] * ext_scr[HALO - 3:HALO - 3 + t, :]
    for tap in range(1, SSD_CONV):
        conv = conv + cw_ref[tap:tap + 1, :] * ext_scr[HALO - 3 + tap:HALO - 3 + tap + t, :]
    ext_scr[0:HALO, :] = ext_scr[t:t + HALO, :]
    act_scr[...] = conv * _sigmoid(conv)

    lane_row = lax.broadcasted_iota(jnp.int32, (1, LANES), 1)
    a_head = jnp.where(lane_row < SSD_HEADS, -jnp.exp(alog_ref[...]), 0.0)
    dt_in = misc_ref[0] + dtb_ref[...]
    dt_scr[...] = jnp.maximum(dt_in, 0.0) + jnp.log1p(jnp.exp(-jnp.abs(dt_in)))

    row_i = lax.broadcasted_iota(jnp.int32, (CHUNK, CHUNK), 0)
    col_i = lax.broadcasted_iota(jnp.int32, (CHUNK, CHUNK), 1)
    causal = row_i >= col_i
    tril = jnp.where(causal, 1.0, 0.0).astype(BF16)
    lane_c = lax.broadcasted_iota(jnp.int32, (CHUNK, LANES), 1)
    left = lane_c < SSD_HEAD_DIM

    def chunk_body(c, carry):
        r0 = pl.multiple_of(c * CHUNK, CHUNK)
        rows = pl.ds(r0, CHUNK)
        dt = dt_scr[rows, :]
        a = dt * a_head
        a_hi = a.astype(BF16)
        a_lo = (a - a_hi.astype(F32)).astype(BF16)
        a_cs = _dot(tril, a_hi) + _dot(tril, a_lo)
        a_cs_t = a_cs.T
        dt_t = dt.T
        last = a_cs[CHUNK - 1:CHUNK, :]
        decay_end_dt = jnp.exp(last - a_cs) * dt
        exp_a = jnp.exp(a_cs)
        w_exp = _dot(decay_end_dt.astype(BF16), expand_ref[...])
        ea_exp = _dot(exp_a.astype(BF16), expand_ref[...])

        xs = act_scr[rows, 0:D_SSD]
        xs_b = xs.astype(BF16)
        x_w = (xs * w_exp).astype(BF16)

        heads_per_group = SSD_HEADS // SSD_GROUPS
        outs = []
        for g in range(SSD_GROUPS):
            bm = act_scr[rows, D_SSD + g * SSD_STATE:D_SSD + (g + 1) * SSD_STATE]
            cm = act_scr[rows, D_SSD + (SSD_GROUPS + g) * SSD_STATE:D_SSD + (SSD_GROUPS + g + 1) * SSD_STATE]
            cm_b = cm.astype(BF16)
            cb = _dot_nt(cm_b, bm.astype(BF16))
            y_pairs = []
            for pair in range(heads_per_group // 2):
                k0 = g * heads_per_group + 2 * pair
                ms = []
                for k in (k0, k0 + 1):
                    seg = a_cs[:, k:k + 1] - a_cs_t[k:k + 1, :]
                    decay = jnp.exp(jnp.where(causal, seg, -jnp.inf))
                    ms.append((cb * decay * dt_t[k:k + 1, :]).astype(BF16))
                xp = xs_b[:, k0 * SSD_HEAD_DIM:k0 * SSD_HEAD_DIM + LANES]
                zero = jnp.zeros_like(xp)
                block_diag = jnp.concatenate([jnp.where(left, xp, zero), jnp.where(left, zero, xp)], axis=0)
                y_pairs.append(_dot(jnp.concatenate(ms, axis=1), block_diag))
            y_diag = jnp.concatenate(y_pairs, axis=1)
            gs = slice(g * GROUP_W, (g + 1) * GROUP_W)
            prev = state_scr[g]
            y_off = _dot(cm_b, prev.astype(BF16)) * ea_exp[:, gs]
            state_scr[g] = prev * ea_exp[CHUNK - 1:CHUNK, gs] + _dot(bm.T.astype(BF16), x_w[:, gs])

            y = y_diag + y_off + xs[:, gs] * dskip_ref[:, gs]
            zc = z_ref[0, rows, gs].astype(F32)
            y = y * (zc * _sigmoid(zc))
            outs.append(_rms(y) * norm_ref[:, gs])
        y_ref[0, rows, :] = jnp.concatenate(outs, axis=1).astype(BF16)
        return carry

    lax.fori_loop(0, n_chunks, chunk_body, 0)


def _ssd_mixer(z, xbc, misc, cw, cb, dtb, alog, dskip, norm, expand):
    bsz, seq, _ = z.shape
    t = min(T_SSD, seq)
    tok = lambda w: pl.BlockSpec((1, t, w), lambda b, s: (b, s, 0))
    return pl.pallas_call(
        _ssd_kernel,
        grid=(bsz, seq // t),
        in_specs=[tok(D_SSD), tok(D_XBC), tok(LANES),
                  _const_spec((SSD_CONV, D_XBC)), _const_spec((1, D_XBC)),
                  _const_spec((1, LANES)), _const_spec((1, LANES)),
                  _const_spec((1, D_SSD)), _const_spec((1, D_SSD)),
                  _const_spec((LANES, D_SSD))],
        out_specs=tok(D_SSD),
        out_shape=jax.ShapeDtypeStruct((bsz, seq, D_SSD), BF16),
        scratch_shapes=[pltpu.VMEM((t + HALO, D_XBC), F32),
                        pltpu.VMEM((t, D_XBC), F32),
                        pltpu.VMEM((t, LANES), F32),
                        pltpu.VMEM((SSD_GROUPS, SSD_STATE, GROUP_W), F32)],
        compiler_params=_params(2),
        name="ssd_mixer",
    )(z, xbc, misc, cw, cb, dtb, alog, dskip, norm, expand)


def _attn_kernel(q_ref, k_ref, vt_ref, o_ref, m_scr, l_scr, acc_scr):
    tq = q_ref.shape[2]
    qi = pl.program_id(1)
    kv_i = lax.broadcasted_iota(jnp.int32, (tq, tq), 0)
    q_i = lax.broadcasted_iota(jnp.int32, (tq, tq), 1)
    causal = q_i >= kv_i

    def scores(hd, j):
        return _dot_nt(k_ref[0, hd, pl.ds(pl.multiple_of(j * tq, tq), tq), :], q_ref[0, hd])

    for hd in range(MLA_HEADS):
        s = jnp.where(causal, scores(hd, qi), -jnp.inf)
        m = jnp.max(s, axis=0, keepdims=True)
        p = jnp.exp2(s - m)
        m_scr[hd] = m
        l_scr[hd] = jnp.sum(p, axis=0, keepdims=True)
        acc_scr[hd] = _dot(vt_ref[0, hd, qi], p.astype(BF16))

    def kv_step(j, carry):
        for hd in range(MLA_HEADS):
            s = scores(hd, j)
            m_prev = m_scr[hd]
            m_new = jnp.maximum(m_prev, jnp.max(s, axis=0, keepdims=True))
            alpha = jnp.exp2(m_prev - m_new)
            p = jnp.exp2(s - m_new)
            m_scr[hd] = m_new
            l_scr[hd] = alpha * l_scr[hd] + jnp.sum(p, axis=0, keepdims=True)
            acc_scr[hd] = alpha * acc_scr[hd] + _dot(vt_ref[0, hd, j], p.astype(BF16))
        return carry

    lax.fori_loop(0, qi, kv_step, 0)
    for hd in range(MLA_HEADS):
        o_ref[0, :, hd * V_DIM:(hd + 1) * V_DIM] = (acc_scr[hd] / l_scr[hd]).T.astype(BF16)


def _attention(q, k, vt):
    bsz, _, seq, _ = q.shape
    tq = vt.shape[-1]
    return pl.pallas_call(
        _attn_kernel,
        grid=(bsz, seq // tq),
        in_specs=[pl.BlockSpec((1, MLA_HEADS, tq, QK_PAD), lambda b, i: (b, 0, i, 0)),
                  pl.BlockSpec((1, MLA_HEADS, seq, QK_PAD), lambda b, i: (b, 0, 0, 0)),
                  pl.BlockSpec((1, MLA_HEADS, seq // tq, V_DIM, tq), lambda b, i: (b, 0, 0, 0, 0))],
        out_specs=pl.BlockSpec((1, tq, D_ATT), lambda b, i: (b, i, 0)),
        out_shape=jax.ShapeDtypeStruct((bsz, seq, D_ATT), BF16),
        scratch_shapes=[pltpu.VMEM((MLA_HEADS, 1, tq), F32),
                        pltpu.VMEM((MLA_HEADS, 1, tq), F32),
                        pltpu.VMEM((MLA_HEADS, V_DIM, tq), F32)],
        compiler_params=_params(2),
        name="mla_attention",
    )(q, k, vt)


def _out_ffn_kernel(final, y_ref, o_ref, x_ref, mod_ref, anorm_ref, wout_ref, gmlp_ref, wup_ref, cw_ref, cb_ref,
                    wdown_ref, fnorm_ref, out_ref, u_scr, act_scr):
    tm = x_ref.shape[1]

    @pl.when(pl.program_id(1) == 0)
    def _():
        u_scr[0:HALO, :] = jnp.zeros((HALO, 2 * D_FF), F32)

    gate1 = mod_ref[:, 2 * D_MODEL:3 * D_MODEL]
    shift2 = mod_ref[:, 3 * D_MODEL:4 * D_MODEL]
    scale2 = mod_ref[:, 4 * D_MODEL:5 * D_MODEL]
    gate2 = mod_ref[:, 5 * D_MODEL:6 * D_MODEL]

    o_n = (_rms(o_ref[0].astype(F32)) * anorm_ref[...]).astype(BF16)
    y = _dot(y_ref[0], wout_ref[0:D_SSD, :]) + _dot(o_n, wout_ref[D_SSD:D_SSD + D_ATT, :])
    x1 = x_ref[0] + gate1 * y
    hb = ((_rms(x1) * gmlp_ref[...]) * (1.0 + scale2) + shift2).astype(BF16)

    base = HALO - (FF_CONV - 1)
    for c in range(D_FF // FF_CHUNK):
        conv = []
        for off in (c * FF_CHUNK, D_FF + c * FF_CHUNK):
            cols = slice(off, off + FF_CHUNK)
            u_scr[HALO:HALO + tm, cols] = _dot(hb, wup_ref[:, cols])
            acc = cb_ref[:, cols] + cw_ref[0:1, cols] * u_scr[base:base + tm, cols]
            for tap in range(1, FF_CONV):
                acc = acc + cw_ref[tap:tap + 1, cols] * u_scr[base + tap:base + tap + tm, cols]
            conv.append(acc)
        act_scr[:, c * FF_CHUNK:(c + 1) * FF_CHUNK] = (conv[0] * _sigmoid(conv[0]) * conv[1]).astype(BF16)
    u_scr[0:HALO, :] = u_scr[tm:tm + HALO, :]

    x2 = x1 + gate2 * _dot(act_scr[...], wdown_ref[...])
    if final:
        x2 = _rms(x2) * fnorm_ref[...]
    out_ref[0] = x2


def _out_ffn(final, y_ssd, o, x, mod_l, anorm, wout, gmlp, wup, cw, cb, wdown, fnorm):
    bsz, seq, _ = x.shape
    tm = min(TM_FFN, seq)
    tok = lambda w: pl.BlockSpec((1, tm, w), lambda b, s: (b, s, 0))
    return pl.pallas_call(
        functools.partial(_out_ffn_kernel, final),
        grid=(bsz, seq // tm),
        in_specs=[tok(D_SSD), tok(D_ATT), tok(D_MODEL),
                  pl.BlockSpec((None, 1, 6 * D_MODEL), lambda b, s: (b, 0, 0)),
                  _const_spec((1, D_ATT)),
                  _const_spec((D_SSD + D_ATT, D_MODEL)),
                  _const_spec((1, D_MODEL)),
                  _const_spec((D_MODEL, 2 * D_FF)),
                  _const_spec((FF_CONV, 2 * D_FF)),
                  _const_spec((1, 2 * D_FF)),
                  _const_spec((D_FF, D_MODEL)),
                  _const_spec((1, D_MODEL))],
        out_specs=tok(D_MODEL),
        out_shape=jax.ShapeDtypeStruct((bsz, seq, D_MODEL), F32),
        scratch_shapes=[pltpu.VMEM((tm + HALO, 2 * D_FF), F32),
                        pltpu.VMEM((tm, D_FF), BF16)],
        compiler_params=_params(2),
        name="out_ffn_final" if final else "out_ffn",
    )(y_ssd, o, x, mod_l, anorm, wout, gmlp, wup, cw, cb, wdown, fnorm)


def _pad_lanes(v):
    return jnp.pad(v.astype(F32), (0, LANES - v.shape[0]))[None, :]


def kernel(x, c, positions, w_ada, b_ada, norm_mix, w_in, conv_w, conv_b, dt_bias, a_log, d_skip, ssd_norm,
           q_norm, w_uq, kv_norm, w_ukv, attn_norm, w_out, norm_mlp, w_up, conv_ff_w, conv_ff_b, w_down,
           final_norm):
    depth = w_in.shape[0]
    bsz = x.shape[0]
    mod = _modulation(c, w_ada, b_ada).reshape(depth, bsz, 1, 6 * D_MODEL)
    cos, sin = _rope_tables(positions)

    expand = jnp.asarray(np.kron(np.eye(LANES, SSD_HEADS, dtype=np.float32),
                                 np.ones((1, SSD_HEAD_DIM), np.float32)), BF16)

    s_z, s_xbc, s_dt = D_SSD, D_SSD + D_XBC, D_SSD + D_XBC + SSD_HEADS
    s_cq, s_ckv = s_dt + Q_RANK, s_dt + Q_RANK + KV_RANK

    for l in range(depth):
        wi = w_in[l]
        win = jnp.concatenate(
            [wi[:, :s_z], wi[:, s_z:s_xbc], wi[:, s_dt:s_cq], wi[:, s_cq:s_ckv], wi[:, s_xbc:s_dt],
             jnp.zeros((D_MODEL, LANES - SSD_HEADS - QK_ROPE), F32), wi[:, s_ckv:]], axis=1).astype(BF16)
        uq = w_uq[l].reshape(Q_RANK, MLA_HEADS, QK_NOPE + QK_ROPE)
        wuq = jnp.concatenate([uq[:, :, :QK_NOPE].reshape(Q_RANK, -1), uq[:, :, QK_NOPE:].reshape(Q_RANK, -1)],
                              axis=1).astype(BF16)
        ukv = w_ukv[l].reshape(KV_RANK, MLA_HEADS, QK_NOPE + V_DIM)
        wukv = jnp.concatenate([ukv[:, :, :QK_NOPE].reshape(KV_RANK, -1), ukv[:, :, QK_NOPE:].reshape(KV_RANK, -1)],
                               axis=1).astype(BF16)

        z, xbc, misc, q, k, vt = _in_proj(x, mod[l], norm_mix[l][None, :], win, q_norm[l][None, :], wuq,
                                          kv_norm[l][None, :], wukv, cos, sin)
        y_ssd = _ssd_mixer(z, xbc, misc, conv_w[l], conv_b[l][None, :], _pad_lanes(dt_bias[l]), _pad_lanes(a_log[l]),
                           jnp.repeat(d_skip[l], SSD_HEAD_DIM)[None, :], ssd_norm[l][None, :], expand)
        o = _attention(q, k, vt)
        x = _out_ffn(l == depth - 1, y_ssd, o, x, mod[l], attn_norm[l][None, :], w_out[l].astype(BF16),
                     norm_mlp[l][None, :], w_up[l].astype(BF16), conv_ff_w[l], conv_ff_b[l][None, :],
                     w_down[l].astype(BF16), final_norm[None, :])
    return x
```

```python
import functools

import jax
import jax.numpy as jnp
import numpy as np
from jax import lax
from jax.experimental import pallas as pl
from jax.experimental.pallas import tpu as pltpu

F32 = jnp.float32
BF16 = jnp.bfloat16

LANES = 128
D_MODEL = 1024
D_SSD = 1024
SSD_HEADS = 16
SSD_HEAD_DIM = 64
SSD_GROUPS = 2
SSD_STATE = 128
SSD_CONV = 4
CHUNK = 128
D_XBC = D_SSD + 2 * SSD_GROUPS * SSD_STATE
GROUP_W = D_SSD // SSD_GROUPS
MLA_HEADS = 8
QK_NOPE = 128
QK_ROPE = 64
QK_PAD = 256
V_DIM = 128
D_ATT = MLA_HEADS * V_DIM
Q_RANK = 384
KV_RANK = 256
ROPE_BASE = 10000.0
D_FF = 2816
FF_CONV = 3
FF_CHUNK = 256
EPS = 1e-6
HALO = 8
HALO_B = 16

OFF_Z = 0
OFF_XBC = OFF_Z + D_SSD
OFF_CQ = OFF_XBC + D_XBC
OFF_CKV = OFF_CQ + Q_RANK
OFF_MISC = OFF_CKV + KV_RANK
D_IN_PAD = OFF_MISC + LANES

VMEM_LIMIT = 56 * 1024 * 1024

T_ATT = 256
TM_IN = 512
T_SSD = 512
TM_FFN = 512

Q_PRESCALE = float((QK_NOPE + QK_ROPE) ** -0.5 * np.log2(np.e))


def _dot(a, b):
    return jnp.dot(a, b, preferred_element_type=F32)


def _dot_nt(a, b):
    return lax.dot_general(a, b, (((1,), (1,)), ((), ())), preferred_element_type=F32)


def _sigmoid(v):
    return 1.0 / (1.0 + jnp.exp(-v))


def _rms(v):
    return v * lax.rsqrt(jnp.mean(v * v, axis=-1, keepdims=True) + EPS)


def _const_spec(shape):
    nd = len(shape)
    return pl.BlockSpec(shape, lambda *_: (0,) * nd, pipeline_mode=pl.Buffered(1))


def _params(n_axes):
    return pltpu.CompilerParams(dimension_semantics=("arbitrary",) * n_axes,
                                vmem_limit_bytes=VMEM_LIMIT)


def _mod_kernel(c_ref, w_ref, b_ref, o_ref):
    c = c_ref[...]
    c_act = (c * _sigmoid(c)).astype(BF16)
    o_ref[0] = _dot(c_act, w_ref[0].astype(BF16)) + b_ref[0]


def _modulation(c, w_ada, b_ada):
    depth, _, n_out = w_ada.shape
    bsz = c.shape[0]
    nblk = n_out // D_MODEL
    return pl.pallas_call(
        _mod_kernel,
        grid=(depth, nblk),
        in_specs=[pl.BlockSpec((bsz, D_MODEL), lambda l, j: (0, 0)),
                  pl.BlockSpec((1, D_MODEL, D_MODEL), lambda l, j: (l, 0, j)),
                  pl.BlockSpec((1, 1, D_MODEL), lambda l, j: (l, 0, j))],
        out_specs=pl.BlockSpec((1, bsz, D_MODEL), lambda l, j: (l, 0, j)),
        out_shape=jax.ShapeDtypeStruct((depth, bsz, n_out), F32),
        compiler_params=_params(2),
        name="adaln_mod",
    )(c, w_ada, b_ada.reshape(depth, 1, n_out))


def _rope_kernel(pos_ref, freq_ref, sign_ref, cos_ref, sin_ref):
    ang = pos_ref[0] * freq_ref[...]
    cos_ref[0] = jnp.cos(ang)
    sin_ref[0] = jnp.sin(ang) * sign_ref[...]


def _rope_tables(positions):
    bsz, seq = positions.shape
    half = QK_ROPE // 2
    inv_freq = 1.0 / (ROPE_BASE ** (np.arange(0, QK_ROPE, 2, dtype=np.float32) / QK_ROPE))
    freq = jnp.asarray(np.tile(inv_freq, LANES // half)[None, :], F32)
    sign = jnp.asarray(np.tile(np.concatenate([-np.ones(half), np.ones(half)]), LANES // QK_ROPE)[None, :], F32)
    pos = jnp.broadcast_to(positions.astype(F32)[..., None], (bsz, seq, LANES))
    tile = min(seq, 1024)
    spec = pl.BlockSpec((1, tile, LANES), lambda b, s: (b, s, 0))
    row = pl.BlockSpec((1, LANES), lambda b, s: (0, 0))
    return pl.pallas_call(
        _rope_kernel,
        grid=(bsz, seq // tile),
        in_specs=[spec, row, row],
        out_specs=[spec, spec],
        out_shape=[jax.ShapeDtypeStruct((bsz, seq, LANES), F32)] * 2,
        compiler_params=_params(2),
        name="rope_tables",
    )(pos, freq, sign)


def _in_kernel(x_ref, mod_ref, gmix_ref, win_ref, qn_ref, wuq_ref, kvn_ref, wukv_ref, cos_ref, sin_ref,
               z_ref, xbc_ref, misc_ref, q_ref, k_ref, vt_ref):
    tm = x_ref.shape[1]
    shift = mod_ref[:, 0:D_MODEL]
    scale = mod_ref[:, D_MODEL:2 * D_MODEL]
    h = (_rms(x_ref[0]) * gmix_ref[...]) * (1.0 + scale) + shift
    hb = h.astype(BF16)

    z_ref[0] = _dot(hb, win_ref[:, OFF_Z:OFF_XBC]).astype(BF16)
    xbc_ref[0] = _dot(hb, win_ref[:, OFF_XBC:OFF_CQ]).astype(BF16)
    cq = _dot(hb, win_ref[:, OFF_CQ:OFF_CKV])
    ckv = _dot(hb, win_ref[:, OFF_CKV:OFF_MISC])
    misc = _dot(hb, win_ref[:, OFF_MISC:D_IN_PAD])
    misc_ref[0] = misc

    cos = cos_ref[0]
    sin = sin_ref[0]
    lane = lax.broadcasted_iota(jnp.int32, (tm, LANES), 1)
    first_half = (lane & (QK_ROPE // 2)) == 0
    low_half = lane < QK_ROPE
    zero = jnp.zeros((tm, LANES), F32)

    def rope(r):
        swapped = jnp.where(first_half, pltpu.roll(r, LANES - QK_ROPE // 2, 1), pltpu.roll(r, QK_ROPE // 2, 1))
        return r * cos + swapped * sin

    q = _dot((_rms(cq) * qn_ref[...]).astype(BF16), wuq_ref[...]) * Q_PRESCALE
    kv = _dot((_rms(ckv) * kvn_ref[...]).astype(BF16), wukv_ref[...])

    k_rope = jnp.where(low_half, pltpu.roll(rope(misc), QK_ROPE, 1), zero).astype(BF16)
    nope_w = MLA_HEADS * QK_NOPE
    for pair in range(MLA_HEADS // 2):
        rq = rope(q[:, nope_w + pair * LANES:nope_w + (pair + 1) * LANES])
        q_ref[0, 2 * pair, :, QK_NOPE:QK_PAD] = jnp.where(low_half, rq, zero).astype(BF16)
        q_ref[0, 2 * pair + 1, :, QK_NOPE:QK_PAD] = jnp.where(low_half, pltpu.roll(rq, QK_ROPE, 1), zero).astype(BF16)
    for hd in range(MLA_HEADS):
        q_ref[0, hd, :, 0:QK_NOPE] = q[:, hd * QK_NOPE:(hd + 1) * QK_NOPE].astype(BF16)
        k_ref[0, hd, :, 0:QK_NOPE] = kv[:, hd * QK_NOPE:(hd + 1) * QK_NOPE].astype(BF16)
        k_ref[0, hd, :, QK_NOPE:QK_PAD] = k_rope
        for sub in range(tm // T_ATT):
            vt_ref[0, hd, sub] = kv[sub * T_ATT:(sub + 1) * T_ATT,
                                    nope_w + hd * V_DIM:nope_w + (hd + 1) * V_DIM].T.astype(BF16)


def _in_proj(x, mod_l, gmix, win, qn, wuq, kvn, wukv, cos, sin):
    bsz, seq, _ = x.shape
    tm = min(TM_IN, seq)
    n_sub = tm // T_ATT
    tok = lambda w: pl.BlockSpec((1, tm, w), lambda b, s: (b, s, 0))
    head = lambda w: pl.BlockSpec((1, MLA_HEADS, tm, w), lambda b, s: (b, 0, s, 0))
    return pl.pallas_call(
        _in_kernel,
        grid=(bsz, seq // tm),
        in_specs=[tok(D_MODEL),
                  pl.BlockSpec((None, 1, 6 * D_MODEL), lambda b, s: (b, 0, 0)),
                  _const_spec((1, D_MODEL)),
                  _const_spec((D_MODEL, D_IN_PAD)),
                  _const_spec((1, Q_RANK)),
                  _const_spec((Q_RANK, MLA_HEADS * (QK_NOPE + QK_ROPE))),
                  _const_spec((1, KV_RANK)),
                  _const_spec((KV_RANK, MLA_HEADS * (QK_NOPE + V_DIM))),
                  tok(LANES), tok(LANES)],
        out_specs=[tok(D_SSD), tok(D_XBC), tok(LANES), head(QK_PAD), head(QK_PAD),
                   pl.BlockSpec((1, MLA_HEADS, n_sub, V_DIM, T_ATT), lambda b, s: (b, 0, s, 0, 0))],
        out_shape=[jax.ShapeDtypeStruct((bsz, seq, D_SSD), BF16),
                   jax.ShapeDtypeStruct((bsz, seq, D_XBC), BF16),
                   jax.ShapeDtypeStruct((bsz, seq, LANES), F32),
                   jax.ShapeDtypeStruct((bsz, MLA_HEADS, seq, QK_PAD), BF16),
                   jax.ShapeDtypeStruct((bsz, MLA_HEADS, seq, QK_PAD), BF16),
                   jax.ShapeDtypeStruct((bsz, MLA_HEADS, seq // T_ATT, V_DIM, T_ATT), BF16)],
        compiler_params=_params(2),
        name="in_proj",
    )(x, mod_l, gmix, win, qn, wuq, kvn, wukv, cos, sin)


def _ssd_kernel(z_ref, xbc_ref, misc_ref, shift_ref, cw_ref, cb_ref, dtb_ref, alog_ref, dskip_ref, norm_ref,
                expand_ref, y_ref, ext_scr, act_scr, state_scr):
    t = z_ref.shape[1]
    n_chunks = t // CHUNK

    @pl.when(pl.program_id(1) == 0)
    def _():
        ext_scr[0:HALO_B, :] = jnp.zeros((HALO_B, D_XBC), BF16)
        state_scr[...] = jnp.zeros(state_scr.shape, F32)

    ext_scr[HALO_B:HALO_B + t, :] = xbc_ref[0]

    lane_row = lax.broadcasted_iota(jnp.int32, (1, LANES), 1)
    a_head = jnp.where(lane_row < SSD_HEADS, -jnp.exp(alog_ref[...]), 0.0)

    row_i = lax.broadcasted_iota(jnp.int32, (CHUNK, CHUNK), 0)
    col_i = lax.broadcasted_iota(jnp.int32, (CHUNK, CHUNK), 1)
    causal = row_i >= col_i
    tril = jnp.where(causal, 1.0, 0.0).astype(BF16)
    lane_c = lax.broadcasted_iota(jnp.int32, (CHUNK, LANES), 1)
    left = lane_c < SSD_HEAD_DIM

    def chunk_body(c, carry):
        r0 = pl.multiple_of(c * CHUNK, CHUNK)
        rows = pl.ds(r0, CHUNK)

        u2 = ext_scr[pl.ds(r0, HALO_B + CHUNK), :]
        delayed = _dot(shift_ref[...], u2)
        conv = cb_ref[...] + cw_ref[SSD_CONV - 1:SSD_CONV, :] * u2[HALO_B:HALO_B + CHUNK, :].astype(F32)
        for tap in range(SSD_CONV - 1):
            conv = conv + cw_ref[tap:tap + 1, :] * delayed[tap * CHUNK:(tap + 1) * CHUNK, :]
        act_scr[...] = conv * _sigmoid(conv)

        dt_in = misc_ref[0, rows, :] + dtb_ref[...]
        dt = jnp.maximum(dt_in, 0.0) + jnp.log1p(jnp.exp(-jnp.abs(dt_in)))
        a = dt * a_head
        a_hi = a.astype(BF16)
        a_lo = (a - a_hi.astype(F32)).astype(BF16)
        a_cs = _dot(tril, a_hi) + _dot(tril, a_lo)
        a_cs_t = a_cs.T
        dt_t = dt.T
        last = a_cs[CHUNK - 1:CHUNK, :]
        decay_end_dt = (jnp.exp(last - a_cs) * dt).astype(BF16)
        exp_a = jnp.exp(a_cs).astype(BF16)

        heads_per_group = SSD_HEADS // SSD_GROUPS
        for g in range(SSD_GROUPS):
            gs = slice(g * GROUP_W, (g + 1) * GROUP_W)
            xs = act_scr[:, gs]
            xs_b = xs.astype(BF16)
            x_w = (xs * _dot(decay_end_dt, expand_ref[:, gs])).astype(BF16)
            ea_exp = _dot(exp_a, expand_ref[:, gs])
            bm = act_scr[:, D_SSD + g * SSD_STATE:D_SSD + (g + 1) * SSD_STATE]
            cm = act_scr[:, D_SSD + (SSD_GROUPS + g) * SSD_STATE:D_SSD + (SSD_GROUPS + g + 1) * SSD_STATE]
            cm_b = cm.astype(BF16)
            cb = _dot_nt(cm_b, bm.astype(BF16))
            y_pairs = []
            for pair in range(heads_per_group // 2):
                k0 = g * heads_per_group + 2 * pair
                ms = []
                for k in (k0, k0 + 1):
                    seg = a_cs[:, k:k + 1] - a_cs_t[k:k + 1, :]
                    decay = jnp.exp(jnp.where(causal, seg, -jnp.inf))
                    ms.append((cb * decay * dt_t[k:k + 1, :]).astype(BF16))
                xp = xs_b[:, 2 * pair * SSD_HEAD_DIM:2 * pair * SSD_HEAD_DIM + LANES]
                zero = jnp.zeros_like(xp)
                block_diag = jnp.concatenate([jnp.where(left, xp, zero), jnp.where(left, zero, xp)], axis=0)
                y_pairs.append(_dot(jnp.concatenate(ms, axis=1), block_diag))
            y_diag = jnp.concatenate(y_pairs, axis=1)
            prev = state_scr[g]
            y_off = _dot(cm_b, prev.astype(BF16)) * ea_exp
            state_scr[g] = prev * ea_exp[CHUNK - 1:CHUNK, :] + _dot(bm.T.astype(BF16), x_w)

            y = y_diag + y_off + xs * dskip_ref[:, gs]
            zc = z_ref[0, rows, gs].astype(F32)
            y = y * (zc * _sigmoid(zc))
            y_ref[0, rows, gs] = (_rms(y) * norm_ref[:, gs]).astype(BF16)
        return carry

    lax.fori_loop(0, n_chunks, chunk_body, 0)
    ext_scr[0:HALO_B, :] = ext_scr[t:t + HALO_B, :]


def _conv_shift_matrix():
    m = np.zeros(((SSD_CONV - 1) * CHUNK, HALO_B + CHUNK), np.float32)
    for tap in range(SSD_CONV - 1):
        delay = SSD_CONV - 1 - tap
        m[tap * CHUNK + np.arange(CHUNK), HALO_B + np.arange(CHUNK) - delay] = 1.0
    return jnp.asarray(m, BF16)


def _ssd_mixer(z, xbc, misc, cw, cb, dtb, alog, dskip, norm, expand):
    bsz, seq, _ = z.shape
    t = min(T_SSD, seq)
    tok = lambda w: pl.BlockSpec((1, t, w), lambda b, s: (b, s, 0))
    return pl.pallas_call(
        _ssd_kernel,
        grid=(bsz, seq // t),
        in_specs=[tok(D_SSD), tok(D_XBC), tok(LANES),
                  _const_spec(((SSD_CONV - 1) * CHUNK, HALO_B + CHUNK)),
                  _const_spec((SSD_CONV, D_XBC)), _const_spec((1, D_XBC)),
                  _const_spec((1, LANES)), _const_spec((1, LANES)),
                  _const_spec((1, D_SSD)), _const_spec((1, D_SSD)),
                  _const_spec((LANES, D_SSD))],
        out_specs=tok(D_SSD),
        out_shape=jax.ShapeDtypeStruct((bsz, seq, D_SSD), BF16),
        scratch_shapes=[pltpu.VMEM((t + HALO_B, D_XBC), BF16),
                        pltpu.VMEM((CHUNK, D_XBC), F32),
                        pltpu.VMEM((SSD_GROUPS, SSD_STATE, GROUP_W), F32)],
        compiler_params=_params(2),
        name="ssd_mixer",
    )(z, xbc, misc, _conv_shift_matrix(), cw, cb, dtb, alog, dskip, norm, expand)


def _attn_kernel(q_ref, k_ref, vt_ref, o_ref, s_scr, m_scr, l_scr, acc_scr):
    tq = q_ref.shape[2]
    ck = 2 * tq
    qi = pl.program_id(1)
    last = qi // 2

    def qk(c, slot):
        rows = pl.ds(pl.multiple_of(c * ck, ck), ck)
        for hd in range(MLA_HEADS):
            s_scr[slot, hd] = _dot_nt(k_ref[0, hd, rows, :], q_ref[0, hd])

    def softmax_pv(c, slot, masked):
        if masked:
            kv_abs = c * ck + lax.broadcasted_iota(jnp.int32, (ck, tq), 0)
            q_abs = qi * tq + lax.broadcasted_iota(jnp.int32, (ck, tq), 1)
            keep = kv_abs <= q_abs
        for hd in range(MLA_HEADS):
            s = s_scr[slot, hd]
            if masked:
                s = jnp.where(keep, s, -jnp.inf)
            m_prev = m_scr[hd]
            m_new = jnp.maximum(m_prev, jnp.max(s, axis=0, keepdims=True))
            alpha = jnp.exp2(m_prev - m_new)
            p = jnp.exp2(s - m_new)
            m_scr[hd] = m_new
            l_scr[hd] = alpha * l_scr[hd] + jnp.sum(p, axis=0, keepdims=True)
            pb = p.astype(BF16)
            pv = _dot(vt_ref[0, hd, 2 * c], pb[0:tq]) + _dot(vt_ref[0, hd, 2 * c + 1], pb[tq:ck])
            acc_scr[hd] = alpha * acc_scr[hd] + pv

    m_scr[...] = jnp.full(m_scr.shape, -jnp.inf, F32)
    l_scr[...] = jnp.zeros(l_scr.shape, F32)
    acc_scr[...] = jnp.zeros(acc_scr.shape, F32)

    qk(0, 0)

    def body(c, carry):
        qk(c + 1, lax.rem(c + 1, 2))
        softmax_pv(c, lax.rem(c, 2), False)
        return carry

    lax.fori_loop(0, last, body, 0)
    softmax_pv(last, lax.rem(last, 2), True)
    for hd in range(MLA_HEADS):
        o_ref[0, :, hd * V_DIM:(hd + 1) * V_DIM] = (acc_scr[hd] / l_scr[hd]).T.astype(BF16)


def _attention(q, k, vt):
    bsz, _, seq, _ = q.shape
    tq = vt.shape[-1]
    assert seq % (2 * tq) == 0, "keys are consumed in chunks of two query tiles"
    return pl.pallas_call(
        _attn_kernel,
        grid=(bsz, seq // tq),
        in_specs=[pl.BlockSpec((1, MLA_HEADS, tq, QK_PAD), lambda b, i: (b, 0, i, 0)),
                  pl.BlockSpec((1, MLA_HEADS, seq, QK_PAD), lambda b, i: (b, 0, 0, 0)),
                  pl.BlockSpec((1, MLA_HEADS, seq // tq, V_DIM, tq), lambda b, i: (b, 0, 0, 0, 0))],
        out_specs=pl.BlockSpec((1, tq, D_ATT), lambda b, i: (b, i, 0)),
        out_shape=jax.ShapeDtypeStruct((bsz, seq, D_ATT), BF16),
        scratch_shapes=[pltpu.VMEM((2, MLA_HEADS, 2 * tq, tq), F32),
                        pltpu.VMEM((MLA_HEADS, 1, tq), F32),
                        pltpu.VMEM((MLA_HEADS, 1, tq), F32),
                        pltpu.VMEM((MLA_HEADS, V_DIM, tq), F32)],
        compiler_params=_params(2),
        name="mla_attention",
    )(q, k, vt)


def _out_ffn_kernel(final, y_ref, o_ref, x_ref, mod_ref, anorm_ref, wout_ref, gmlp_ref, wup_ref, cw_ref, cb_ref,
                    wdown_ref, fnorm_ref, out_ref, u_scr, act_scr):
    tm = x_ref.shape[1]

    @pl.when(pl.program_id(1) == 0)
    def _():
        u_scr[0:HALO, :] = jnp.zeros((HALO, 2 * D_FF), F32)

    gate1 = mod_ref[:, 2 * D_MODEL:3 * D_MODEL]
    shift2 = mod_ref[:, 3 * D_MODEL:4 * D_MODEL]
    scale2 = mod_ref[:, 4 * D_MODEL:5 * D_MODEL]
    gate2 = mod_ref[:, 5 * D_MODEL:6 * D_MODEL]

    o_n = (_rms(o_ref[0].astype(F32)) * anorm_ref[...]).astype(BF16)
    y = _dot(y_ref[0], wout_ref[0:D_SSD, :]) + _dot(o_n, wout_ref[D_SSD:D_SSD + D_ATT, :])
    x1 = x_ref[0] + gate1 * y
    hb = ((_rms(x1) * gmlp_ref[...]) * (1.0 + scale2) + shift2).astype(BF16)

    base = HALO - (FF_CONV - 1)
    for c in range(D_FF // FF_CHUNK):
        conv = []
        for off in (c * FF_CHUNK, D_FF + c * FF_CHUNK):
            cols = slice(off, off + FF_CHUNK)
            u_scr[HALO:HALO + tm, cols] = _dot(hb, wup_ref[:, cols])
            acc = cb_ref[:, cols] + cw_ref[0:1, cols] * u_scr[base:base + tm, cols]
            for tap in range(1, FF_CONV):
                acc = acc + cw_ref[tap:tap + 1, cols] * u_scr[base + tap:base + tap + tm, cols]
            conv.append(acc)
        act_scr[:, c * FF_CHUNK:(c + 1) * FF_CHUNK] = (conv[0] * _sigmoid(conv[0]) * conv[1]).astype(BF16)
    u_scr[0:HALO, :] = u_scr[tm:tm + HALO, :]

    x2 = x1 + gate2 * _dot(act_scr[...], wdown_ref[...])
    if final:
        x2 = _rms(x2) * fnorm_ref[...]
    out_ref[0] = x2


def _out_ffn(final, y_ssd, o, x, mod_l, anorm, wout, gmlp, wup, cw, cb, wdown, fnorm):
    bsz, seq, _ = x.shape
    tm = min(TM_FFN, seq)
    tok = lambda w: pl.BlockSpec((1, tm, w), lambda b, s: (b, s, 0))
    return pl.pallas_call(
        functools.partial(_out_ffn_kernel, final),
        grid=(bsz, seq // tm),
        in_specs=[tok(D_SSD), tok(D_ATT), tok(D_MODEL),
                  pl.BlockSpec((None, 1, 6 * D_MODEL), lambda b, s: (b, 0, 0)),
                  _const_spec((1, D_ATT)),
                  _const_spec((D_SSD + D_ATT, D_MODEL)),
                  _const_spec((1, D_MODEL)),
                  _const_spec((D_MODEL, 2 * D_FF)),
                  _const_spec((FF_CONV, 2 * D_FF)),
                  _const_spec((1, 2 * D_FF)),
                  _const_spec((D_FF, D_MODEL)),
                  _const_spec((1, D_MODEL))],
        out_specs=tok(D_MODEL),
        out_shape=jax.ShapeDtypeStruct((bsz, seq, D_MODEL), F32),
        scratch_shapes=[pltpu.VMEM((tm + HALO, 2 * D_FF), F32),
                        pltpu.VMEM((tm, D_FF), BF16)],
        compiler_params=_params(2),
        name="out_ffn_final" if final else "out_ffn",
    )(y_ssd, o, x, mod_l, anorm, wout, gmlp, wup, cw, cb, wdown, fnorm)


def _pad_lanes(v):
    return jnp.pad(v.astype(F32), (0, LANES - v.shape[0]))[None, :]


def kernel(x, c, positions, w_ada, b_ada, norm_mix, w_in, conv_w, conv_b, dt_bias, a_log, d_skip, ssd_norm,
           q_norm, w_uq, kv_norm, w_ukv, attn_norm, w_out, norm_mlp, w_up, conv_ff_w, conv_ff_b, w_down,
           final_norm):
    depth = w_in.shape[0]
    bsz = x.shape[0]
    mod = _modulation(c, w_ada, b_ada).reshape(depth, bsz, 1, 6 * D_MODEL)
    cos, sin = _rope_tables(positions)

    expand = jnp.asarray(np.kron(np.eye(LANES, SSD_HEADS, dtype=np.float32),
                                 np.ones((1, SSD_HEAD_DIM), np.float32)), BF16)

    s_z, s_xbc, s_dt = D_SSD, D_SSD + D_XBC, D_SSD + D_XBC + SSD_HEADS
    s_cq, s_ckv = s_dt + Q_RANK, s_dt + Q_RANK + KV_RANK

    for l in range(depth):
        wi = w_in[l]
        win = jnp.concatenate(
            [wi[:, :s_z], wi[:, s_z:s_xbc], wi[:, s_dt:s_cq], wi[:, s_cq:s_ckv], wi[:, s_xbc:s_dt],
             jnp.zeros((D_MODEL, LANES - SSD_HEADS - QK_ROPE), F32), wi[:, s_ckv:]], axis=1).astype(BF16)
        uq = w_uq[l].reshape(Q_RANK, MLA_HEADS, QK_NOPE + QK_ROPE)
        wuq = jnp.concatenate([uq[:, :, :QK_NOPE].reshape(Q_RANK, -1), uq[:, :, QK_NOPE:].reshape(Q_RANK, -1)],
                              axis=1).astype(BF16)
        ukv = w_ukv[l].reshape(KV_RANK, MLA_HEADS, QK_NOPE + V_DIM)
        wukv = jnp.concatenate([ukv[:, :, :QK_NOPE].reshape(KV_RANK, -1), ukv[:, :, QK_NOPE:].reshape(KV_RANK, -1)],
                               axis=1).astype(BF16)

        z, xbc, misc, q, k, vt = _in_proj(x, mod[l], norm_mix[l][None, :], win, q_norm[l][None, :], wuq,
                                          kv_norm[l][None, :], wukv, cos, sin)
        y_ssd = _ssd_mixer(z, xbc, misc, conv_w[l], conv_b[l][None, :], _pad_lanes(dt_bias[l]), _pad_lanes(a_log[l]),
                           jnp.repeat(d_skip[l], SSD_HEAD_DIM)[None, :], ssd_norm[l][None, :], expand)
        o = _attention(q, k, vt)
        x = _out_ffn(l == depth - 1, y_ssd, o, x, mod[l], attn_norm[l][None, :], w_out[l].astype(BF16),
                     norm_mlp[l][None, :], w_up[l].astype(BF16), conv_ff_w[l], conv_ff_b[l][None, :],
                     w_down[l].astype(BF16), final_norm[None, :])
    return x
```

```python
import functools

import jax
import jax.numpy as jnp
import numpy as np
from jax import lax
from jax.experimental import pallas as pl
from jax.experimental.pallas import tpu as pltpu

F32 = jnp.float32
BF16 = jnp.bfloat16

LANES = 128
D_MODEL = 1024
D_SSD = 1024
SSD_HEADS = 16
SSD_HEAD_DIM = 64
SSD_GROUPS = 2
SSD_STATE = 128
SSD_CONV = 4
CHUNK = 128
D_XBC = D_SSD + 2 * SSD_GROUPS * SSD_STATE
GROUP_W = D_SSD // SSD_GROUPS
MLA_HEADS = 8
QK_NOPE = 128
QK_ROPE = 64
QK_PAD = 256
V_DIM = 128
D_ATT = MLA_HEADS * V_DIM
Q_RANK = 384
KV_RANK = 256
ROPE_BASE = 10000.0
D_FF = 2816
FF_CONV = 3
FF_CHUNK = 256
EPS = 1e-6
HALO = 8
HALO_B = 16

OFF_Z = 0
OFF_XBC = OFF_Z + D_SSD
OFF_CQ = OFF_XBC + D_XBC
OFF_CKV = OFF_CQ + Q_RANK
OFF_MISC = OFF_CKV + KV_RANK
D_IN_PAD = OFF_MISC + LANES

VMEM_LIMIT = 56 * 1024 * 1024

T_ATT = 512
TM_IN = 512
T_SSD = 512
TM_FFN = 512

Q_PRESCALE = float((QK_NOPE + QK_ROPE) ** -0.5 * np.log2(np.e))


def _dot(a, b):
    return jnp.dot(a, b, preferred_element_type=F32)


def _dot_nt(a, b):
    return lax.dot_general(a, b, (((1,), (1,)), ((), ())), preferred_element_type=F32)


def _sigmoid(v):
    return 1.0 / (1.0 + jnp.exp(-v))


def _rms(v):
    return v * lax.rsqrt(jnp.mean(v * v, axis=-1, keepdims=True) + EPS)


def _const_spec(shape):
    nd = len(shape)
    return pl.BlockSpec(shape, lambda *_: (0,) * nd, pipeline_mode=pl.Buffered(1))


def _params(n_axes):
    return pltpu.CompilerParams(dimension_semantics=("arbitrary",) * n_axes,
                                vmem_limit_bytes=VMEM_LIMIT)


def _mod_kernel(c_ref, w_ref, b_ref, o_ref):
    c = c_ref[...]
    c_act = (c * _sigmoid(c)).astype(BF16)
    o_ref[0] = _dot(c_act, w_ref[0].astype(BF16)) + b_ref[0]


def _modulation(c, w_ada, b_ada):
    depth, _, n_out = w_ada.shape
    bsz = c.shape[0]
    nblk = n_out // D_MODEL
    return pl.pallas_call(
        _mod_kernel,
        grid=(depth, nblk),
        in_specs=[pl.BlockSpec((bsz, D_MODEL), lambda l, j: (0, 0)),
                  pl.BlockSpec((1, D_MODEL, D_MODEL), lambda l, j: (l, 0, j)),
                  pl.BlockSpec((1, 1, D_MODEL), lambda l, j: (l, 0, j))],
        out_specs=pl.BlockSpec((1, bsz, D_MODEL), lambda l, j: (l, 0, j)),
        out_shape=jax.ShapeDtypeStruct((depth, bsz, n_out), F32),
        compiler_params=_params(2),
        name="adaln_mod",
    )(c, w_ada, b_ada.reshape(depth, 1, n_out))


def _rope_kernel(pos_ref, freq_ref, sign_ref, cos_ref, sin_ref):
    ang = pos_ref[0] * freq_ref[...]
    cos_ref[0] = jnp.cos(ang)
    sin_ref[0] = jnp.sin(ang) * sign_ref[...]


def _rope_tables(positions):
    bsz, seq = positions.shape
    half = QK_ROPE // 2
    inv_freq = 1.0 / (ROPE_BASE ** (np.arange(0, QK_ROPE, 2, dtype=np.float32) / QK_ROPE))
    freq = jnp.asarray(np.tile(inv_freq, LANES // half)[None, :], F32)
    sign = jnp.asarray(np.tile(np.concatenate([-np.ones(half), np.ones(half)]), LANES // QK_ROPE)[None, :], F32)
    pos = jnp.broadcast_to(positions.astype(F32)[..., None], (bsz, seq, LANES))
    tile = min(seq, 1024)
    spec = pl.BlockSpec((1, tile, LANES), lambda b, s: (b, s, 0))
    row = pl.BlockSpec((1, LANES), lambda b, s: (0, 0))
    return pl.pallas_call(
        _rope_kernel,
        grid=(bsz, seq // tile),
        in_specs=[spec, row, row],
        out_specs=[spec, spec],
        out_shape=[jax.ShapeDtypeStruct((bsz, seq, LANES), F32)] * 2,
        compiler_params=_params(2),
        name="rope_tables",
    )(pos, freq, sign)


def _in_kernel(x_ref, mod_ref, gmix_ref, win_ref, qn_ref, wuq_ref, kvn_ref, wukv_ref, cos_ref, sin_ref,
               z_ref, xbc_ref, misc_ref, q_ref, k_ref, vt_ref):
    tm = x_ref.shape[1]
    shift = mod_ref[:, 0:D_MODEL]
    scale = mod_ref[:, D_MODEL:2 * D_MODEL]
    h = (_rms(x_ref[0]) * gmix_ref[...]) * (1.0 + scale) + shift
    hb = h.astype(BF16)

    z_ref[0] = _dot(hb, win_ref[:, OFF_Z:OFF_XBC]).astype(BF16)
    xbc_ref[0] = _dot(hb, win_ref[:, OFF_XBC:OFF_CQ]).astype(BF16)
    cq = _dot(hb, win_ref[:, OFF_CQ:OFF_CKV])
    ckv = _dot(hb, win_ref[:, OFF_CKV:OFF_MISC])
    misc = _dot(hb, win_ref[:, OFF_MISC:D_IN_PAD])
    misc_ref[0] = misc

    cos = cos_ref[0]
    sin = sin_ref[0]
    lane = lax.broadcasted_iota(jnp.int32, (tm, LANES), 1)
    first_half = (lane & (QK_ROPE // 2)) == 0
    low_half = lane < QK_ROPE
    zero = jnp.zeros((tm, LANES), F32)

    def rope(r):
        swapped = jnp.where(first_half, pltpu.roll(r, LANES - QK_ROPE // 2, 1), pltpu.roll(r, QK_ROPE // 2, 1))
        return r * cos + swapped * sin

    q = _dot((_rms(cq) * qn_ref[...]).astype(BF16), wuq_ref[...]) * Q_PRESCALE
    kv = _dot((_rms(ckv) * kvn_ref[...]).astype(BF16), wukv_ref[...])

    k_rope = jnp.where(low_half, pltpu.roll(rope(misc), QK_ROPE, 1), zero).astype(BF16)
    nope_w = MLA_HEADS * QK_NOPE
    for pair in range(MLA_HEADS // 2):
        rq = rope(q[:, nope_w + pair * LANES:nope_w + (pair + 1) * LANES])
        q_ref[0, 2 * pair, :, QK_NOPE:QK_PAD] = jnp.where(low_half, rq, zero).astype(BF16)
        q_ref[0, 2 * pair + 1, :, QK_NOPE:QK_PAD] = jnp.where(low_half, pltpu.roll(rq, QK_ROPE, 1), zero).astype(BF16)
    for hd in range(MLA_HEADS):
        q_ref[0, hd, :, 0:QK_NOPE] = q[:, hd * QK_NOPE:(hd + 1) * QK_NOPE].astype(BF16)
        k_ref[0, hd, :, 0:QK_NOPE] = kv[:, hd * QK_NOPE:(hd + 1) * QK_NOPE].astype(BF16)
        k_ref[0, hd, :, QK_NOPE:QK_PAD] = k_rope
        for sub in range(tm // T_ATT):
            vt_ref[0, hd, sub] = kv[sub * T_ATT:(sub + 1) * T_ATT,
                                    nope_w + hd * V_DIM:nope_w + (hd + 1) * V_DIM].T.astype(BF16)


def _in_proj(x, mod_l, gmix, win, qn, wuq, kvn, wukv, cos, sin):
    bsz, seq, _ = x.shape
    tm = min(TM_IN, seq)
    n_sub = tm // T_ATT
    tok = lambda w: pl.BlockSpec((1, tm, w), lambda b, s: (b, s, 0))
    head = lambda w: pl.BlockSpec((1, MLA_HEADS, tm, w), lambda b, s: (b, 0, s, 0))
    return pl.pallas_call(
        _in_kernel,
        grid=(bsz, seq // tm),
        in_specs=[tok(D_MODEL),
                  pl.BlockSpec((None, 1, 6 * D_MODEL), lambda b, s: (b, 0, 0)),
                  _const_spec((1, D_MODEL)),
                  _const_spec((D_MODEL, D_IN_PAD)),
                  _const_spec((1, Q_RANK)),
                  _const_spec((Q_RANK, MLA_HEADS * (QK_NOPE + QK_ROPE))),
                  _const_spec((1, KV_RANK)),
                  _const_spec((KV_RANK, MLA_HEADS * (QK_NOPE + V_DIM))),
                  tok(LANES), tok(LANES)],
        out_specs=[tok(D_SSD), tok(D_XBC), tok(LANES), head(QK_PAD), head(QK_PAD),
                   pl.BlockSpec((1, MLA_HEADS, n_sub, V_DIM, T_ATT), lambda b, s: (b, 0, s, 0, 0))],
        out_shape=[jax.ShapeDtypeStruct((bsz, seq, D_SSD), BF16),
                   jax.ShapeDtypeStruct((bsz, seq, D_XBC), BF16),
                   jax.ShapeDtypeStruct((bsz, seq, LANES), F32),
                   jax.ShapeDtypeStruct((bsz, MLA_HEADS, seq, QK_PAD), BF16),
                   jax.ShapeDtypeStruct((bsz, MLA_HEADS, seq, QK_PAD), BF16),
                   jax.ShapeDtypeStruct((bsz, MLA_HEADS, seq // T_ATT, V_DIM, T_ATT), BF16)],
        compiler_params=_params(2),
        name="in_proj",
    )(x, mod_l, gmix, win, qn, wuq, kvn, wukv, cos, sin)


def _ssd_kernel(z_ref, xbc_ref, misc_ref, shift_ref, cw_ref, cb_ref, dtb_ref, alog_ref, dskip_ref, norm_ref,
                expand_ref, y_ref, ext_scr, act_scr, state_scr):
    t = z_ref.shape[1]
    n_chunks = t // CHUNK

    @pl.when(pl.program_id(1) == 0)
    def _():
        ext_scr[0:HALO_B, :] = jnp.zeros((HALO_B, D_XBC), BF16)
        state_scr[...] = jnp.zeros(state_scr.shape, F32)

    ext_scr[HALO_B:HALO_B + t, :] = xbc_ref[0]

    lane_row = lax.broadcasted_iota(jnp.int32, (1, LANES), 1)
    a_head = jnp.where(lane_row < SSD_HEADS, -jnp.exp(alog_ref[...]), 0.0)

    row_i = lax.broadcasted_iota(jnp.int32, (CHUNK, CHUNK), 0)
    col_i = lax.broadcasted_iota(jnp.int32, (CHUNK, CHUNK), 1)
    causal = row_i >= col_i
    tril = jnp.where(causal, 1.0, 0.0).astype(BF16)
    lane_c = lax.broadcasted_iota(jnp.int32, (CHUNK, LANES), 1)
    left = lane_c < SSD_HEAD_DIM

    def chunk_body(c, carry):
        r0 = pl.multiple_of(c * CHUNK, CHUNK)
        rows = pl.ds(r0, CHUNK)

        u2 = ext_scr[pl.ds(r0, HALO_B + CHUNK), :]
        delayed = _dot(shift_ref[...], u2)
        conv = cb_ref[...] + cw_ref[SSD_CONV - 1:SSD_CONV, :] * u2[HALO_B:HALO_B + CHUNK, :].astype(F32)
        for tap in range(SSD_CONV - 1):
            conv = conv + cw_ref[tap:tap + 1, :] * delayed[tap * CHUNK:(tap + 1) * CHUNK, :]
        act_scr[...] = conv * _sigmoid(conv)

        dt_in = misc_ref[0, rows, :] + dtb_ref[...]
        dt = jnp.maximum(dt_in, 0.0) + jnp.log1p(jnp.exp(-jnp.abs(dt_in)))
        a = dt * a_head
        a_hi = a.astype(BF16)
        a_lo = (a - a_hi.astype(F32)).astype(BF16)
        a_cs = _dot(tril, a_hi) + _dot(tril, a_lo)
        a_cs_t = a_cs.T
        dt_t = dt.T
        last = a_cs[CHUNK - 1:CHUNK, :]
        decay_end_dt = (jnp.exp(last - a_cs) * dt).astype(BF16)
        exp_a = jnp.exp(a_cs).astype(BF16)

        heads_per_group = SSD_HEADS // SSD_GROUPS
        for g in range(SSD_GROUPS):
            gs = slice(g * GROUP_W, (g + 1) * GROUP_W)
            xs = act_scr[:, gs]
            xs_b = xs.astype(BF16)
            x_w = (xs * _dot(decay_end_dt, expand_ref[:, gs])).astype(BF16)
            ea_exp = _dot(exp_a, expand_ref[:, gs])
            bm = act_scr[:, D_SSD + g * SSD_STATE:D_SSD + (g + 1) * SSD_STATE]
            cm = act_scr[:, D_SSD + (SSD_GROUPS + g) * SSD_STATE:D_SSD + (SSD_GROUPS + g + 1) * SSD_STATE]
            cm_b = cm.astype(BF16)
            cb = _dot_nt(cm_b, bm.astype(BF16))
            y_pairs = []
            for pair in range(heads_per_group // 2):
                k0 = g * heads_per_group + 2 * pair
                ms = []
                for k in (k0, k0 + 1):
                    seg = a_cs[:, k:k + 1] - a_cs_t[k:k + 1, :]
                    decay = jnp.exp(jnp.where(causal, seg, -jnp.inf))
                    ms.append((cb * decay * dt_t[k:k + 1, :]).astype(BF16))
                xp = xs_b[:, 2 * pair * SSD_HEAD_DIM:2 * pair * SSD_HEAD_DIM + LANES]
                zero = jnp.zeros_like(xp)
                block_diag = jnp.concatenate([jnp.where(left, xp, zero), jnp.where(left, zero, xp)], axis=0)
                y_pairs.append(_dot(jnp.concatenate(ms, axis=1), block_diag))
            y_diag = jnp.concatenate(y_pairs, axis=1)
            prev = state_scr[g]
            y_off = _dot(cm_b, prev.astype(BF16)) * ea_exp
            state_scr[g] = prev * ea_exp[CHUNK - 1:CHUNK, :] + _dot(bm.T.astype(BF16), x_w)

            y = y_diag + y_off + xs * dskip_ref[:, gs]
            zc = z_ref[0, rows, gs].astype(F32)
            y = y * (zc * _sigmoid(zc))
            y_ref[0, rows, gs] = (_rms(y) * norm_ref[:, gs]).astype(BF16)
        return carry

    lax.fori_loop(0, n_chunks, chunk_body, 0)
    ext_scr[0:HALO_B, :] = ext_scr[t:t + HALO_B, :]


def _conv_shift_matrix():
    m = np.zeros(((SSD_CONV - 1) * CHUNK, HALO_B + CHUNK), np.float32)
    for tap in range(SSD_CONV - 1):
        delay = SSD_CONV - 1 - tap
        m[tap * CHUNK + np.arange(CHUNK), HALO_B + np.arange(CHUNK) - delay] = 1.0
    return jnp.asarray(m, BF16)


def _ssd_mixer(z, xbc, misc, cw, cb, dtb, alog, dskip, norm, expand):
    bsz, seq, _ = z.shape
    t = min(T_SSD, seq)
    tok = lambda w: pl.BlockSpec((1, t, w), lambda b, s: (b, s, 0))
    return pl.pallas_call(
        _ssd_kernel,
        grid=(bsz, seq // t),
        in_specs=[tok(D_SSD), tok(D_XBC), tok(LANES),
                  _const_spec(((SSD_CONV - 1) * CHUNK, HALO_B + CHUNK)),
                  _const_spec((SSD_CONV, D_XBC)), _const_spec((1, D_XBC)),
                  _const_spec((1, LANES)), _const_spec((1, LANES)),
                  _const_spec((1, D_SSD)), _const_spec((1, D_SSD)),
                  _const_spec((LANES, D_SSD))],
        out_specs=tok(D_SSD),
        out_shape=jax.ShapeDtypeStruct((bsz, seq, D_SSD), BF16),
        scratch_shapes=[pltpu.VMEM((t + HALO_B, D_XBC), BF16),
                        pltpu.VMEM((CHUNK, D_XBC), F32),
                        pltpu.VMEM((SSD_GROUPS, SSD_STATE, GROUP_W), F32)],
        compiler_params=_params(2),
        name="ssd_mixer",
    )(z, xbc, misc, _conv_shift_matrix(), cw, cb, dtb, alog, dskip, norm, expand)


def _attn_kernel(q_ref, k_ref, vt_ref, o_ref, s_scr, m_scr, l_scr, acc_scr):
    tq = q_ref.shape[2]
    qi = pl.program_id(1)
    keep = lax.broadcasted_iota(jnp.int32, (tq, tq), 0) <= lax.broadcasted_iota(jnp.int32, (tq, tq), 1)

    def qk(c, hd):
        rows = pl.ds(pl.multiple_of(c * tq, tq), tq)
        s_scr[hd % 2] = _dot_nt(k_ref[0, hd, rows, :], q_ref[0, hd])

    def softmax_pv(c, hd, masked):
        s = s_scr[hd % 2]
        if masked:
            s = jnp.where(keep, s, -jnp.inf)
        m_prev = m_scr[hd]
        m_new = jnp.maximum(m_prev, jnp.max(s, axis=0, keepdims=True))
        alpha = jnp.exp2(m_prev - m_new)
        p = jnp.exp2(s - m_new)
        m_scr[hd] = m_new
        l_scr[hd] = alpha * l_scr[hd] + jnp.sum(p, axis=0, keepdims=True)
        acc_scr[hd] = alpha * acc_scr[hd] + _dot(vt_ref[0, hd, c], p.astype(BF16))

    def chunk(c, masked, prefetch_next_chunk):
        for hd in range(MLA_HEADS):
            if hd + 1 < MLA_HEADS:
                qk(c, hd + 1)
            elif prefetch_next_chunk:
                qk(c + 1, 0)
            softmax_pv(c, hd, masked)

    m_scr[...] = jnp.full(m_scr.shape, -jnp.inf, F32)
    l_scr[...] = jnp.zeros(l_scr.shape, F32)
    acc_scr[...] = jnp.zeros(acc_scr.shape, F32)

    qk(0, 0)

    def body(c, carry):
        chunk(c, False, True)
        return carry

    lax.fori_loop(0, qi, body, 0)
    chunk(qi, True, False)
    for hd in range(MLA_HEADS):
        o_ref[0, :, hd * V_DIM:(hd + 1) * V_DIM] = (acc_scr[hd] / l_scr[hd]).T.astype(BF16)


def _attention(q, k, vt):
    bsz, _, seq, _ = q.shape
    tq = vt.shape[-1]
    return pl.pallas_call(
        _attn_kernel,
        grid=(bsz, seq // tq),
        in_specs=[pl.BlockSpec((1, MLA_HEADS, tq, QK_PAD), lambda b, i: (b, 0, i, 0)),
                  pl.BlockSpec((1, MLA_HEADS, seq, QK_PAD), lambda b, i: (b, 0, 0, 0)),
                  pl.BlockSpec((1, MLA_HEADS, seq // tq, V_DIM, tq), lambda b, i: (b, 0, 0, 0, 0))],
        out_specs=pl.BlockSpec((1, tq, D_ATT), lambda b, i: (b, i, 0)),
        out_shape=jax.ShapeDtypeStruct((bsz, seq, D_ATT), BF16),
        scratch_shapes=[pltpu.VMEM((2, tq, tq), F32),
                        pltpu.VMEM((MLA_HEADS, 1, tq), F32),
                        pltpu.VMEM((MLA_HEADS, 1, tq), F32),
                        pltpu.VMEM((MLA_HEADS, V_DIM, tq), F32)],
        compiler_params=_params(2),
        name="mla_attention",
    )(q, k, vt)


def _out_ffn_kernel(final, y_ref, o_ref, x_ref, mod_ref, anorm_ref, wout_ref, gmlp_ref, wup_ref, cw_ref, cb_ref,
                    wdown_ref, fnorm_ref, out_ref, u_scr, act_scr):
    tm = x_ref.shape[1]

    @pl.when(pl.program_id(1) == 0)
    def _():
        u_scr[0:HALO, :] = jnp.zeros((HALO, 2 * D_FF), F32)

    gate1 = mod_ref[:, 2 * D_MODEL:3 * D_MODEL]
    shift2 = mod_ref[:, 3 * D_MODEL:4 * D_MODEL]
    scale2 = mod_ref[:, 4 * D_MODEL:5 * D_MODEL]
    gate2 = mod_ref[:, 5 * D_MODEL:6 * D_MODEL]

    o_n = (_rms(o_ref[0].astype(F32)) * anorm_ref[...]).astype(BF16)
    y = _dot(y_ref[0], wout_ref[0:D_SSD, :]) + _dot(o_n, wout_ref[D_SSD:D_SSD + D_ATT, :])
    x1 = x_ref[0] + gate1 * y
    hb = ((_rms(x1) * gmlp_ref[...]) * (1.0 + scale2) + shift2).astype(BF16)

    base = HALO - (FF_CONV - 1)
    for c in range(D_FF // FF_CHUNK):
        conv = []
        for off in (c * FF_CHUNK, D_FF + c * FF_CHUNK):
            cols = slice(off, off + FF_CHUNK)
            u_scr[HALO:HALO + tm, cols] = _dot(hb, wup_ref[:, cols])
            acc = cb_ref[:, cols] + cw_ref[0:1, cols] * u_scr[base:base + tm, cols]
            for tap in range(1, FF_CONV):
                acc = acc + cw_ref[tap:tap + 1, cols] * u_scr[base + tap:base + tap + tm, cols]
            conv.append(acc)
        act_scr[:, c * FF_CHUNK:(c + 1) * FF_CHUNK] = (conv[0] * _sigmoid(conv[0]) * conv[1]).astype(BF16)
    u_scr[0:HALO, :] = u_scr[tm:tm + HALO, :]

    x2 = x1 + gate2 * _dot(act_scr[...], wdown_ref[...])
    if final:
        x2 = _rms(x2) * fnorm_ref[...]
    out_ref[0] = x2


def _out_ffn(final, y_ssd, o, x, mod_l, anorm, wout, gmlp, wup, cw, cb, wdown, fnorm):
    bsz, seq, _ = x.shape
    tm = min(TM_FFN, seq)
    tok = lambda w: pl.BlockSpec((1, tm, w), lambda b, s: (b, s, 0))
    return pl.pallas_call(
        functools.partial(_out_ffn_kernel, final),
        grid=(bsz, seq // tm),
        in_specs=[tok(D_SSD), tok(D_ATT), tok(D_MODEL),
                  pl.BlockSpec((None, 1, 6 * D_MODEL), lambda b, s: (b, 0, 0)),
                  _const_spec((1, D_ATT)),
                  _const_spec((D_SSD + D_ATT, D_MODEL)),
                  _const_spec((1, D_MODEL)),
                  _const_spec((D_MODEL, 2 * D_FF)),
                  _const_spec((FF_CONV, 2 * D_FF)),
                  _const_spec((1, 2 * D_FF)),
                  _const_spec((D_FF, D_MODEL)),
                  _const_spec((1, D_MODEL))],
        out_specs=tok(D_MODEL),
        out_shape=jax.ShapeDtypeStruct((bsz, seq, D_MODEL), F32),
        scratch_shapes=[pltpu.VMEM((tm + HALO, 2 * D_FF), F32),
                        pltpu.VMEM((tm, D_FF), BF16)],
        compiler_params=_params(2),
        name="out_ffn_final" if final else "out_ffn",
    )(y_ssd, o, x, mod_l, anorm, wout, gmlp, wup, cw, cb, wdown, fnorm)


def _pad_lanes(v):
    return jnp.pad(v.astype(F32), (0, LANES - v.shape[0]))[None, :]


def kernel(x, c, positions, w_ada, b_ada, norm_mix, w_in, conv_w, conv_b, dt_bias, a_log, d_skip, ssd_norm,
           q_norm, w_uq, kv_norm, w_ukv, attn_norm, w_out, norm_mlp, w_up, conv_ff_w, conv_ff_b, w_down,
           final_norm):
    depth = w_in.shape[0]
    bsz = x.shape[0]
    mod = _modulation(c, w_ada, b_ada).reshape(depth, bsz, 1, 6 * D_MODEL)
    cos, sin = _rope_tables(positions)

    expand = jnp.asarray(np.kron(np.eye(LANES, SSD_HEADS, dtype=np.float32),
                                 np.ones((1, SSD_HEAD_DIM), np.float32)), BF16)

    s_z, s_xbc, s_dt = D_SSD, D_SSD + D_XBC, D_SSD + D_XBC + SSD_HEADS
    s_cq, s_ckv = s_dt + Q_RANK, s_dt + Q_RANK + KV_RANK

    for l in range(depth):
        wi = w_in[l]
        win = jnp.concatenate(
            [wi[:, :s_z], wi[:, s_z:s_xbc], wi[:, s_dt:s_cq], wi[:, s_cq:s_ckv], wi[:, s_xbc:s_dt],
             jnp.zeros((D_MODEL, LANES - SSD_HEADS - QK_ROPE), F32), wi[:, s_ckv:]], axis=1).astype(BF16)
        uq = w_uq[l].reshape(Q_RANK, MLA_HEADS, QK_NOPE + QK_ROPE)
        wuq = jnp.concatenate([uq[:, :, :QK_NOPE].reshape(Q_RANK, -1), uq[:, :, QK_NOPE:].reshape(Q_RANK, -1)],
                              axis=1).astype(BF16)
        ukv = w_ukv[l].reshape(KV_RANK, MLA_HEADS, QK_NOPE + V_DIM)
        wukv = jnp.concatenate([ukv[:, :, :QK_NOPE].reshape(KV_RANK, -1), ukv[:, :, QK_NOPE:].reshape(KV_RANK, -1)],
                               axis=1).astype(BF16)

        z, xbc, misc, q, k, vt = _in_proj(x, mod[l], norm_mix[l][None, :], win, q_norm[l][None, :], wuq,
                                          kv_norm[l][None, :], wukv, cos, sin)
        y_ssd = _ssd_mixer(z, xbc, misc, conv_w[l], conv_b[l][None, :], _pad_lanes(dt_bias[l]), _pad_lanes(a_log[l]),
                           jnp.repeat(d_skip[l], SSD_HEAD_DIM)[None, :], ssd_norm[l][None, :], expand)
        o = _attention(q, k, vt)
        x = _out_ffn(l == depth - 1, y_ssd, o, x, mod[l], attn_norm[l][None, :], w_out[l].astype(BF16),
                     norm_mlp[l][None, :], w_up[l].astype(BF16), conv_ff_w[l], conv_ff_b[l][None, :],
                     w_down[l].astype(BF16), final_norm[None, :])
    return x
```

```python
import functools

import jax
import jax.numpy as jnp
import numpy as np
from jax import lax
from jax.experimental import pallas as pl
from jax.experimental.pallas import tpu as pltpu

F32 = jnp.float32
BF16 = jnp.bfloat16

LANES = 128
D_MODEL = 1024
D_SSD = 1024
SSD_HEADS = 16
SSD_HEAD_DIM = 64
SSD_GROUPS = 2
SSD_STATE = 128
SSD_CONV = 4
CHUNK = 128
D_XBC = D_SSD + 2 * SSD_GROUPS * SSD_STATE
GROUP_W = D_SSD // SSD_GROUPS
MLA_HEADS = 8
QK_NOPE = 128
QK_ROPE = 64
QK_PAD = 256
V_DIM = 128
D_ATT = MLA_HEADS * V_DIM
Q_RANK = 384
KV_RANK = 256
ROPE_BASE = 10000.0
D_FF = 2816
FF_CONV = 3
FF_CHUNK = 256
EPS = 1e-6
HALO = 8
HALO_B = 16

OFF_Z = 0
OFF_XBC = OFF_Z + D_SSD
OFF_CQ = OFF_XBC + D_XBC
OFF_CKV = OFF_CQ + Q_RANK
OFF_MISC = OFF_CKV + KV_RANK
D_IN_PAD = OFF_MISC + LANES

VMEM_LIMIT = 56 * 1024 * 1024

T_ATT = 512
TM_IN = 512
T_SSD = 512
TM_FFN = 512

LOG2E = float(np.log2(np.e))
Q_PRESCALE = float((QK_NOPE + QK_ROPE) ** -0.5) * LOG2E


def _dot(a, b):
    return jnp.dot(a, b, preferred_element_type=F32)


def _dot_nt(a, b):
    return lax.dot_general(a, b, (((1,), (1,)), ((), ())), preferred_element_type=F32)


def _silu(v):
    h = 0.5 * v
    return h + h * jnp.tanh(h)


def _rms(v):
    return v * lax.rsqrt(jnp.mean(v * v, axis=-1, keepdims=True) + EPS)


def _const_spec(shape):
    nd = len(shape)
    return pl.BlockSpec(shape, lambda *_: (0,) * nd, pipeline_mode=pl.Buffered(1))


def _params(n_axes):
    return pltpu.CompilerParams(dimension_semantics=("arbitrary",) * n_axes,
                                vmem_limit_bytes=VMEM_LIMIT)


def _mod_kernel(c_ref, w_ref, b_ref, o_ref):
    c = c_ref[...]
    c_act = _silu(c).astype(BF16)
    o_ref[0] = _dot(c_act, w_ref[0].astype(BF16)) + b_ref[0]


def _modulation(c, w_ada, b_ada):
    depth, _, n_out = w_ada.shape
    bsz = c.shape[0]
    nblk = n_out // D_MODEL
    return pl.pallas_call(
        _mod_kernel,
        grid=(depth, nblk),
        in_specs=[pl.BlockSpec((bsz, D_MODEL), lambda l, j: (0, 0)),
                  pl.BlockSpec((1, D_MODEL, D_MODEL), lambda l, j: (l, 0, j)),
                  pl.BlockSpec((1, 1, D_MODEL), lambda l, j: (l, 0, j))],
        out_specs=pl.BlockSpec((1, bsz, D_MODEL), lambda l, j: (l, 0, j)),
        out_shape=jax.ShapeDtypeStruct((depth, bsz, n_out), F32),
        compiler_params=_params(2),
        name="adaln_mod",
    )(c, w_ada, b_ada.reshape(depth, 1, n_out))


def _rope_kernel(pos_ref, freq_ref, cos_ref, sin_ref):
    ang = pos_ref[0] * freq_ref[...]
    cos_ref[0] = jnp.cos(ang)
    sin_ref[0] = jnp.sin(ang)


def _rope_tables(positions):
    bsz, seq = positions.shape
    half = QK_ROPE // 2
    per_row = LANES // half
    inv_freq = 1.0 / (ROPE_BASE ** (np.arange(0, QK_ROPE, 2, dtype=np.float32) / QK_ROPE))
    freq = jnp.asarray(np.tile(inv_freq, per_row)[None, :], F32)
    sign = jnp.asarray(np.tile(np.concatenate([-np.ones(half), np.ones(half)]), LANES // QK_ROPE), F32)
    pos = jnp.repeat(positions.astype(F32).reshape(bsz, seq // per_row, per_row), half, axis=-1)
    rows = seq // per_row
    spec = pl.BlockSpec((1, rows, LANES), lambda b: (b, 0, 0))
    cos_c, sin_c = pl.pallas_call(
        _rope_kernel,
        grid=(bsz,),
        in_specs=[spec, pl.BlockSpec((1, LANES), lambda b: (0, 0))],
        out_specs=[spec, spec],
        out_shape=[jax.ShapeDtypeStruct((bsz, rows, LANES), F32)] * 2,
        compiler_params=_params(1),
        name="rope_tables",
    )(pos, freq)
    cos = jnp.tile(cos_c.reshape(bsz, seq, half), (1, 1, per_row))
    sin = jnp.tile(sin_c.reshape(bsz, seq, half), (1, 1, per_row)) * sign
    return cos, sin


def _in_kernel(x_ref, mod_ref, gmix_ref, win_ref, qn_ref, wuq_ref, kvn_ref, wukv_ref, cos_ref, sin_ref,
               z_ref, xbc_ref, misc_ref, q_ref, k_ref, vt_ref):
    tm = x_ref.shape[1]
    shift = mod_ref[:, 0:D_MODEL]
    scale = mod_ref[:, D_MODEL:2 * D_MODEL]
    h = (_rms(x_ref[0]) * gmix_ref[...]) * (1.0 + scale) + shift
    hb = h.astype(BF16)

    cq = _dot(hb, win_ref[:, OFF_CQ:OFF_CKV])
    ckv = _dot(hb, win_ref[:, OFF_CKV:OFF_MISC])
    misc = _dot(hb, win_ref[:, OFF_MISC:D_IN_PAD])
    misc_ref[0] = misc

    cos = cos_ref[0]
    sin = sin_ref[0]
    lane = lax.broadcasted_iota(jnp.int32, (tm, LANES), 1)
    first_half = (lane & (QK_ROPE // 2)) == 0
    low_half = lane < QK_ROPE
    zero = jnp.zeros((tm, LANES), F32)

    def rope(r):
        swapped = jnp.where(first_half, pltpu.roll(r, LANES - QK_ROPE // 2, 1), pltpu.roll(r, QK_ROPE // 2, 1))
        return r * cos + swapped * sin

    q = _dot((_rms(cq) * qn_ref[...]).astype(BF16), wuq_ref[...]) * Q_PRESCALE
    kv = _dot((_rms(ckv) * kvn_ref[...]).astype(BF16), wukv_ref[...])

    k_rope = jnp.where(low_half, pltpu.roll(rope(misc), QK_ROPE, 1), zero).astype(BF16)
    nope_w = MLA_HEADS * QK_NOPE
    for pair in range(MLA_HEADS // 2):
        rq = rope(q[:, nope_w + pair * LANES:nope_w + (pair + 1) * LANES])
        q_ref[0, 2 * pair, :, QK_NOPE:QK_PAD] = jnp.where(low_half, rq, zero).astype(BF16)
        q_ref[0, 2 * pair + 1, :, QK_NOPE:QK_PAD] = jnp.where(low_half, pltpu.roll(rq, QK_ROPE, 1), zero).astype(BF16)
    for hd in range(MLA_HEADS):
        q_ref[0, hd, :, 0:QK_NOPE] = q[:, hd * QK_NOPE:(hd + 1) * QK_NOPE].astype(BF16)
        k_ref[0, hd, :, 0:QK_NOPE] = kv[:, hd * QK_NOPE:(hd + 1) * QK_NOPE].astype(BF16)
        k_ref[0, hd, :, QK_NOPE:QK_PAD] = k_rope
        for sub in range(tm // T_ATT):
            vt_ref[0, hd, sub] = kv[sub * T_ATT:(sub + 1) * T_ATT,
                                    nope_w + hd * V_DIM:nope_w + (hd + 1) * V_DIM].T.astype(BF16)

    z_ref[0] = _dot(hb, win_ref[:, OFF_Z:OFF_XBC]).astype(BF16)
    xbc_ref[0] = _dot(hb, win_ref[:, OFF_XBC:OFF_CQ]).astype(BF16)


def _in_proj(x, mod_l, gmix, win, qn, wuq, kvn, wukv, cos, sin):
    bsz, seq, _ = x.shape
    tm = min(TM_IN, seq)
    n_sub = tm // T_ATT
    tok = lambda w: pl.BlockSpec((1, tm, w), lambda b, s: (b, s, 0))
    head = lambda w: pl.BlockSpec((1, MLA_HEADS, tm, w), lambda b, s: (b, 0, s, 0))
    return pl.pallas_call(
        _in_kernel,
        grid=(bsz, seq // tm),
        in_specs=[tok(D_MODEL),
                  pl.BlockSpec((None, 1, 6 * D_MODEL), lambda b, s: (b, 0, 0)),
                  _const_spec((1, D_MODEL)),
                  _const_spec((D_MODEL, D_IN_PAD)),
                  _const_spec((1, Q_RANK)),
                  _const_spec((Q_RANK, MLA_HEADS * (QK_NOPE + QK_ROPE))),
                  _const_spec((1, KV_RANK)),
                  _const_spec((KV_RANK, MLA_HEADS * (QK_NOPE + V_DIM))),
                  tok(LANES), tok(LANES)],
        out_specs=[tok(D_SSD), tok(D_XBC), tok(LANES), head(QK_PAD), head(QK_PAD),
                   pl.BlockSpec((1, MLA_HEADS, n_sub, V_DIM, T_ATT), lambda b, s: (b, 0, s, 0, 0))],
        out_shape=[jax.ShapeDtypeStruct((bsz, seq, D_SSD), BF16),
                   jax.ShapeDtypeStruct((bsz, seq, D_XBC), BF16),
                   jax.ShapeDtypeStruct((bsz, seq, LANES), F32),
                   jax.ShapeDtypeStruct((bsz, MLA_HEADS, seq, QK_PAD), BF16),
                   jax.ShapeDtypeStruct((bsz, MLA_HEADS, seq, QK_PAD), BF16),
                   jax.ShapeDtypeStruct((bsz, MLA_HEADS, seq // T_ATT, V_DIM, T_ATT), BF16)],
        compiler_params=_params(2),
        name="in_proj",
    )(x, mod_l, gmix, win, qn, wuq, kvn, wukv, cos, sin)


def _ssd_kernel(z_ref, xbc_ref, misc_ref, shift_ref, cw_ref, cb_ref, dtb_ref, alog_ref, dskip_ref, norm_ref,
                expand_ref, y_ref, ext_scr, act_scr, state_scr):
    t = z_ref.shape[1]
    n_chunks = t // CHUNK

    @pl.when(pl.program_id(1) == 0)
    def _():
        ext_scr[0:HALO_B, :] = jnp.zeros((HALO_B, D_XBC), BF16)
        state_scr[...] = jnp.zeros(state_scr.shape, F32)

    ext_scr[HALO_B:HALO_B + t, :] = xbc_ref[0]

    lane_row = lax.broadcasted_iota(jnp.int32, (1, LANES), 1)
    a_head = jnp.where(lane_row < SSD_HEADS, -jnp.exp(alog_ref[...]) * LOG2E, 0.0)

    row_i = lax.broadcasted_iota(jnp.int32, (CHUNK, CHUNK), 0)
    col_i = lax.broadcasted_iota(jnp.int32, (CHUNK, CHUNK), 1)
    causal = row_i >= col_i
    tril = jnp.where(causal, 1.0, 0.0).astype(BF16)
    lane_c = lax.broadcasted_iota(jnp.int32, (CHUNK, LANES), 1)
    left = lane_c < SSD_HEAD_DIM

    def chunk_body(c, carry):
        r0 = pl.multiple_of(c * CHUNK, CHUNK)
        rows = pl.ds(r0, CHUNK)

        u2 = ext_scr[pl.ds(r0, HALO_B + CHUNK), :]
        delayed = _dot(shift_ref[...], u2)
        conv = cb_ref[...] + cw_ref[SSD_CONV - 1:SSD_CONV, :] * u2[HALO_B:HALO_B + CHUNK, :].astype(F32)
        for tap in range(SSD_CONV - 1):
            conv = conv + cw_ref[tap:tap + 1, :] * delayed[tap * CHUNK:(tap + 1) * CHUNK, :]
        act = act_scr
        act[...] = _silu(conv)

        dt_in = misc_ref[0, rows, :] + dtb_ref[...]
        dt = jnp.maximum(dt_in, 0.0) + jnp.log1p(jnp.exp(-jnp.abs(dt_in)))
        a = dt * a_head
        a_hi = a.astype(BF16)
        a_lo = (a - a_hi.astype(F32)).astype(BF16)
        a_cs = _dot(tril, a_hi) + _dot(tril, a_lo)
        a_cs_t = a_cs.T
        dt_t = dt.T
        last = a_cs[CHUNK - 1:CHUNK, :]
        decay_end_dt = (jnp.exp2(last - a_cs) * dt).astype(BF16)
        exp_a = jnp.exp2(a_cs).astype(BF16)

        heads_per_group = SSD_HEADS // SSD_GROUPS
        for g in range(SSD_GROUPS):
            gs = slice(g * GROUP_W, (g + 1) * GROUP_W)
            xs = act[:, gs]
            xs_b = xs.astype(BF16)
            x_w = (xs * _dot(decay_end_dt, expand_ref[:, gs])).astype(BF16)
            ea_exp = _dot(exp_a, expand_ref[:, gs])
            bm = act[:, D_SSD + g * SSD_STATE:D_SSD + (g + 1) * SSD_STATE]
            cm = act[:, D_SSD + (SSD_GROUPS + g) * SSD_STATE:D_SSD + (SSD_GROUPS + g + 1) * SSD_STATE]
            cm_b = cm.astype(BF16)
            cb = _dot_nt(cm_b, bm.astype(BF16))
            y_pairs = []
            for pair in range(heads_per_group // 2):
                k0 = g * heads_per_group + 2 * pair
                ms = []
                for k in (k0, k0 + 1):
                    seg = a_cs[:, k:k + 1] - a_cs_t[k:k + 1, :]
                    decay = jnp.exp2(jnp.where(causal, seg, -jnp.inf))
                    ms.append((cb * decay * dt_t[k:k + 1, :]).astype(BF16))
                xp = xs_b[:, 2 * pair * SSD_HEAD_DIM:2 * pair * SSD_HEAD_DIM + LANES]
                zero = jnp.zeros_like(xp)
                block_diag = jnp.concatenate([jnp.where(left, xp, zero), jnp.where(left, zero, xp)], axis=0)
                y_pairs.append(_dot(jnp.concatenate(ms, axis=1), block_diag))
            y_diag = jnp.concatenate(y_pairs, axis=1)
            prev = state_scr[g]
            y_off = _dot(cm_b, prev.astype(BF16)) * ea_exp
            state_scr[g] = prev * ea_exp[CHUNK - 1:CHUNK, :] + _dot(bm.T.astype(BF16), x_w)

            y = y_diag + y_off + xs * dskip_ref[:, gs]
            zc = z_ref[0, rows, gs].astype(F32)
            y = y * _silu(zc)
            y_ref[0, rows, gs] = (_rms(y) * norm_ref[:, gs]).astype(BF16)
        return carry

    lax.fori_loop(0, n_chunks, chunk_body, 0)
    ext_scr[0:HALO_B, :] = ext_scr[t:t + HALO_B, :]


def _conv_shift_matrix():
    m = np.zeros(((SSD_CONV - 1) * CHUNK, HALO_B + CHUNK), np.float32)
    for tap in range(SSD_CONV - 1):
        delay = SSD_CONV - 1 - tap
        m[tap * CHUNK + np.arange(CHUNK), HALO_B + np.arange(CHUNK) - delay] = 1.0
    return jnp.asarray(m, BF16)


def _ssd_mixer(z, xbc, misc, cw, cb, dtb, alog, dskip, norm, expand):
    bsz, seq, _ = z.shape
    t = min(T_SSD, seq)
    tok = lambda w: pl.BlockSpec((1, t, w), lambda b, s: (b, s, 0))
    return pl.pallas_call(
        _ssd_kernel,
        grid=(bsz, seq // t),
        in_specs=[tok(D_SSD), tok(D_XBC), tok(LANES),
                  _const_spec(((SSD_CONV - 1) * CHUNK, HALO_B + CHUNK)),
                  _const_spec((SSD_CONV, D_XBC)), _const_spec((1, D_XBC)),
                  _const_spec((1, LANES)), _const_spec((1, LANES)),
                  _const_spec((1, D_SSD)), _const_spec((1, D_SSD)),
                  _const_spec((LANES, D_SSD))],
        out_specs=tok(D_SSD),
        out_shape=jax.ShapeDtypeStruct((bsz, seq, D_SSD), BF16),
        scratch_shapes=[pltpu.VMEM((t + HALO_B, D_XBC), BF16),
                        pltpu.VMEM((CHUNK, D_XBC), F32),
                        pltpu.VMEM((SSD_GROUPS, SSD_STATE, GROUP_W), F32)],
        compiler_params=_params(2),
        name="ssd_mixer",
    )(z, xbc, misc, _conv_shift_matrix(), cw, cb, dtb, alog, dskip, norm, expand)


def _attn_kernel(q_ref, k_ref, vt_ref, o_ref, s_scr, m_scr, l_scr, acc_scr):
    tq = q_ref.shape[2]
    qi = pl.program_id(1)
    keep = lax.broadcasted_iota(jnp.int32, (tq, tq), 0) <= lax.broadcasted_iota(jnp.int32, (tq, tq), 1)

    def qk(c, hd):
        rows = pl.ds(pl.multiple_of(c * tq, tq), tq)
        s_scr[hd % 2] = _dot_nt(k_ref[0, hd, rows, :], q_ref[0, hd])

    def softmax_pv(c, hd, masked):
        s = s_scr[hd % 2]
        if masked:
            s = jnp.where(keep, s, -jnp.inf)
        m_prev = m_scr[hd]
        m_new = jnp.maximum(m_prev, jnp.max(s, axis=0, keepdims=True))
        alpha = jnp.exp2(m_prev - m_new)
        p = jnp.exp2(s - m_new)
        m_scr[hd] = m_new
        l_scr[hd] = alpha * l_scr[hd] + jnp.sum(p, axis=0, keepdims=True)
        acc_scr[hd] = alpha * acc_scr[hd] + _dot(vt_ref[0, hd, c], p.astype(BF16))

    def chunk(c, last):
        for hd in range(MLA_HEADS):
            if hd + 1 < MLA_HEADS:
                qk(c, hd + 1)
            elif not last:
                qk(c + 1, 0)
            softmax_pv(c, hd, last)
            if last:
                o_ref[0, :, hd * V_DIM:(hd + 1) * V_DIM] = (acc_scr[hd] / l_scr[hd]).T.astype(BF16)

    m_scr[...] = jnp.full(m_scr.shape, -jnp.inf, F32)
    l_scr[...] = jnp.zeros(l_scr.shape, F32)
    acc_scr[...] = jnp.zeros(acc_scr.shape, F32)

    qk(0, 0)

    def body(c, carry):
        chunk(c, False)
        return carry

    lax.fori_loop(0, qi, body, 0)
    chunk(qi, True)


def _attention(q, k, vt):
    bsz, _, seq, _ = q.shape
    tq = vt.shape[-1]
    return pl.pallas_call(
        _attn_kernel,
        grid=(bsz, seq // tq),
        in_specs=[pl.BlockSpec((1, MLA_HEADS, tq, QK_PAD), lambda b, i: (b, 0, i, 0)),
                  pl.BlockSpec((1, MLA_HEADS, seq, QK_PAD), lambda b, i: (b, 0, 0, 0)),
                  pl.BlockSpec((1, MLA_HEADS, seq // tq, V_DIM, tq), lambda b, i: (b, 0, 0, 0, 0))],
        out_specs=pl.BlockSpec((1, tq, D_ATT), lambda b, i: (b, i, 0)),
        out_shape=jax.ShapeDtypeStruct((bsz, seq, D_ATT), BF16),
        scratch_shapes=[pltpu.VMEM((2, tq, tq), F32),
                        pltpu.VMEM((MLA_HEADS, 1, tq), F32),
                        pltpu.VMEM((MLA_HEADS, 1, tq), F32),
                        pltpu.VMEM((MLA_HEADS, V_DIM, tq), F32)],
        compiler_params=_params(2),
        name="mla_attention",
    )(q, k, vt)


def _out_ffn_kernel(final, y_ref, o_ref, x_ref, mod_ref, anorm_ref, wout_ref, gmlp_ref, wup_ref, cw_ref, cb_ref,
                    wdown_ref, fnorm_ref, out_ref, u_scr, act_scr):
    tm = x_ref.shape[1]

    @pl.when(pl.program_id(1) == 0)
    def _():
        u_scr[0:HALO, :] = jnp.zeros((HALO, 2 * D_FF), F32)

    gate1 = mod_ref[:, 2 * D_MODEL:3 * D_MODEL]
    shift2 = mod_ref[:, 3 * D_MODEL:4 * D_MODEL]
    scale2 = mod_ref[:, 4 * D_MODEL:5 * D_MODEL]
    gate2 = mod_ref[:, 5 * D_MODEL:6 * D_MODEL]

    o_n = (_rms(o_ref[0].astype(F32)) * anorm_ref[...]).astype(BF16)
    y = _dot(y_ref[0], wout_ref[0:D_SSD, :]) + _dot(o_n, wout_ref[D_SSD:D_SSD + D_ATT, :])
    x1 = x_ref[0] + gate1 * y
    hb = ((_rms(x1) * gmlp_ref[...]) * (1.0 + scale2) + shift2).astype(BF16)

    base = HALO - (FF_CONV - 1)
    for c in range(D_FF // FF_CHUNK):
        conv = []
        for off in (c * FF_CHUNK, D_FF + c * FF_CHUNK):
            cols = slice(off, off + FF_CHUNK)
            u_scr[HALO:HALO + tm, cols] = _dot(hb, wup_ref[:, cols])
            acc = cb_ref[:, cols] + cw_ref[0:1, cols] * u_scr[base:base + tm, cols]
            for tap in range(1, FF_CONV):
                acc = acc + cw_ref[tap:tap + 1, cols] * u_scr[base + tap:base + tap + tm, cols]
            conv.append(acc)
        act_scr[:, c * FF_CHUNK:(c + 1) * FF_CHUNK] = (_silu(conv[0]) * conv[1]).astype(BF16)
    u_scr[0:HALO, :] = u_scr[tm:tm + HALO, :]

    x2 = x1 + gate2 * _dot(act_scr[...], wdown_ref[...])
    if final:
        x2 = _rms(x2) * fnorm_ref[...]
    out_ref[0] = x2


def _out_ffn(final, y_ssd, o, x, mod_l, anorm, wout, gmlp, wup, cw, cb, wdown, fnorm):
    bsz, seq, _ = x.shape
    tm = min(TM_FFN, seq)
    tok = lambda w: pl.BlockSpec((1, tm, w), lambda b, s: (b, s, 0))
    return pl.pallas_call(
        functools.partial(_out_ffn_kernel, final),
        grid=(bsz, seq // tm),
        in_specs=[tok(D_SSD), tok(D_ATT), tok(D_MODEL),
                  pl.BlockSpec((None, 1, 6 * D_MODEL), lambda b, s: (b, 0, 0)),
                  _const_spec((1, D_ATT)),
                  _const_spec((D_SSD + D_ATT, D_MODEL)),
                  _const_spec((1, D_MODEL)),
                  _const_spec((D_MODEL, 2 * D_FF)),
                  _const_spec((FF_CONV, 2 * D_FF)),
                  _const_spec((1, 2 * D_FF)),
                  _const_spec((D_FF, D_MODEL)),
                  _const_spec((1, D_MODEL))],
        out_specs=tok(D_MODEL),
        out_shape=jax.ShapeDtypeStruct((bsz, seq, D_MODEL), F32),
        scratch_shapes=[pltpu.VMEM((tm + HALO, 2 * D_FF), F32),
                        pltpu.VMEM((tm, D_FF), BF16)],
        compiler_params=_params(2),
        name="out_ffn_final" if final else "out_ffn",
    )(y_ssd, o, x, mod_l, anorm, wout, gmlp, wup, cw, cb, wdown, fnorm)


def _pad_lanes(v):
    return jnp.pad(v.astype(F32), (0, LANES - v.shape[0]))[None, :]


def kernel(x, c, positions, w_ada, b_ada, norm_mix, w_in, conv_w, conv_b, dt_bias, a_log, d_skip, ssd_norm,
           q_norm, w_uq, kv_norm, w_ukv, attn_norm, w_out, norm_mlp, w_up, conv_ff_w, conv_ff_b, w_down,
           final_norm):
    depth = w_in.shape[0]
    bsz = x.shape[0]
    mod = _modulation(c, w_ada, b_ada).reshape(depth, bsz, 1, 6 * D_MODEL)
    cos, sin = _rope_tables(positions)

    expand = jnp.asarray(np.kron(np.eye(LANES, SSD_HEADS, dtype=np.float32),
                                 np.ones((1, SSD_HEAD_DIM), np.float32)), BF16)

    s_z, s_xbc, s_dt = D_SSD, D_SSD + D_XBC, D_SSD + D_XBC + SSD_HEADS
    s_cq, s_ckv = s_dt + Q_RANK, s_dt + Q_RANK + KV_RANK

    for l in range(depth):
        wi = w_in[l].astype(BF16)
        win = jnp.concatenate(
            [wi[:, :s_z], wi[:, s_z:s_xbc], wi[:, s_dt:s_cq], wi[:, s_cq:s_ckv], wi[:, s_xbc:s_dt],
             jnp.zeros((D_MODEL, LANES - SSD_HEADS - QK_ROPE), BF16), wi[:, s_ckv:]], axis=1)
        uq = w_uq[l].astype(BF16).reshape(Q_RANK, MLA_HEADS, QK_NOPE + QK_ROPE)
        wuq = jnp.concatenate([uq[:, :, :QK_NOPE].reshape(Q_RANK, -1), uq[:, :, QK_NOPE:].reshape(Q_RANK, -1)],
                              axis=1)
        ukv = w_ukv[l].astype(BF16).reshape(KV_RANK, MLA_HEADS, QK_NOPE + V_DIM)
        wukv = jnp.concatenate([ukv[:, :, :QK_NOPE].reshape(KV_RANK, -1), ukv[:, :, QK_NOPE:].reshape(KV_RANK, -1)],
                               axis=1)

        z, xbc, misc, q, k, vt = _in_proj(x, mod[l], norm_mix[l][None, :], win, q_norm[l][None, :], wuq,
                                          kv_norm[l][None, :], wukv, cos, sin)
        y_ssd = _ssd_mixer(z, xbc, misc, conv_w[l], conv_b[l][None, :], _pad_lanes(dt_bias[l]), _pad_lanes(a_log[l]),
                           jnp.repeat(d_skip[l], SSD_HEAD_DIM)[None, :], ssd_norm[l][None, :], expand)
        o = _attention(q, k, vt)
        x = _out_ffn(l == depth - 1, y_ssd, o, x, mod[l], attn_norm[l][None, :], w_out[l].astype(BF16),
                     norm_mlp[l][None, :], w_up[l].astype(BF16), conv_ff_w[l], conv_ff_b[l][None, :],
                     w_down[l].astype(BF16), final_norm[None, :])
    return x
```

```python
import functools

import jax
import jax.numpy as jnp
import numpy as np
from jax import lax
from jax.experimental import pallas as pl
from jax.experimental.pallas import tpu as pltpu

F32 = jnp.float32
BF16 = jnp.bfloat16

LANES = 128
D_MODEL = 1024
D_SSD = 1024
SSD_HEADS = 16
SSD_HEAD_DIM = 64
SSD_GROUPS = 2
SSD_STATE = 128
SSD_CONV = 4
CHUNK = 128
D_XBC = D_SSD + 2 * SSD_GROUPS * SSD_STATE
GROUP_W = D_SSD // SSD_GROUPS
MLA_HEADS = 8
QK_NOPE = 128
QK_ROPE = 64
QK_PAD = 256
V_DIM = 128
D_ATT = MLA_HEADS * V_DIM
Q_RANK = 384
KV_RANK = 256
ROPE_BASE = 10000.0
D_FF = 2816
FF_CONV = 3
FF_CHUNK = 256
EPS = 1e-6
HALO = 8
HALO_B = 16

OFF_Z = 0
OFF_XBC = OFF_Z + D_SSD
OFF_CQ = OFF_XBC + D_XBC
OFF_CKV = OFF_CQ + Q_RANK
OFF_MISC = OFF_CKV + KV_RANK
D_IN_PAD = OFF_MISC + LANES

VMEM_LIMIT = 56 * 1024 * 1024

T_ATT = 512
TM_IN = 512
T_SSD = 512
TM_FFN = 512

LOG2E = float(np.log2(np.e))
Q_PRESCALE = float((QK_NOPE + QK_ROPE) ** -0.5) * LOG2E

S_AHEAD = 2
S_SLOTS = 4
assert S_SLOTS > S_AHEAD and MLA_HEADS % S_SLOTS == 0


def _dot(a, b):
    return jnp.dot(a, b, preferred_element_type=F32)


def _dot_nt(a, b):
    return lax.dot_general(a, b, (((1,), (1,)), ((), ())), preferred_element_type=F32)


def _silu(v):
    h = 0.5 * v
    return h + h * jnp.tanh(h)


def _rms(v):
    return v * lax.rsqrt(jnp.mean(v * v, axis=-1, keepdims=True) + EPS)


def _const_spec(shape):
    nd = len(shape)
    return pl.BlockSpec(shape, lambda *_: (0,) * nd, pipeline_mode=pl.Buffered(1))


def _params(n_axes):
    return pltpu.CompilerParams(dimension_semantics=("arbitrary",) * n_axes,
                                vmem_limit_bytes=VMEM_LIMIT)


def _mod_kernel(c_ref, w_ref, b_ref, o_ref):
    c = c_ref[...]
    c_act = _silu(c).astype(BF16)
    o_ref[0] = _dot(c_act, w_ref[0].astype(BF16)) + b_ref[0]


def _modulation(c, w_ada, b_ada):
    depth, _, n_out = w_ada.shape
    bsz = c.shape[0]
    nblk = n_out // D_MODEL
    return pl.pallas_call(
        _mod_kernel,
        grid=(depth, nblk),
        in_specs=[pl.BlockSpec((bsz, D_MODEL), lambda l, j: (0, 0)),
                  pl.BlockSpec((1, D_MODEL, D_MODEL), lambda l, j: (l, 0, j)),
                  pl.BlockSpec((1, 1, D_MODEL), lambda l, j: (l, 0, j))],
        out_specs=pl.BlockSpec((1, bsz, D_MODEL), lambda l, j: (l, 0, j)),
        out_shape=jax.ShapeDtypeStruct((depth, bsz, n_out), F32),
        compiler_params=_params(2),
        name="adaln_mod",
    )(c, w_ada, b_ada.reshape(depth, 1, n_out))


def _rope_kernel(pos_ref, freq_ref, sign_ref, cos_ref, sin_ref):
    ang = pos_ref[0] * freq_ref[...]
    cos_ref[0] = jnp.cos(ang)
    sin_ref[0] = jnp.sin(ang) * sign_ref[...]


def _rope_tables(positions):
    bsz, seq = positions.shape
    half = QK_ROPE // 2
    inv_freq = 1.0 / (ROPE_BASE ** (np.arange(0, QK_ROPE, 2, dtype=np.float32) / QK_ROPE))
    freq = jnp.asarray(np.tile(inv_freq, LANES // half)[None, :], F32)
    sign = jnp.asarray(np.tile(np.concatenate([-np.ones(half), np.ones(half)]), LANES // QK_ROPE)[None, :], F32)
    pos = jnp.broadcast_to(positions.astype(F32)[..., None], (bsz, seq, LANES))
    tile = min(seq, 1024)
    spec = pl.BlockSpec((1, tile, LANES), lambda b, s: (b, s, 0))
    row = pl.BlockSpec((1, LANES), lambda b, s: (0, 0))
    return pl.pallas_call(
        _rope_kernel,
        grid=(bsz, seq // tile),
        in_specs=[spec, row, row],
        out_specs=[spec, spec],
        out_shape=[jax.ShapeDtypeStruct((bsz, seq, LANES), F32)] * 2,
        compiler_params=_params(2),
        name="rope_tables",
    )(pos, freq, sign)


def _in_kernel(x_ref, mod_ref, gmix_ref, win_ref, qn_ref, wuq_ref, kvn_ref, wukv_ref, cos_ref, sin_ref,
               z_ref, xbc_ref, misc_ref, q_ref, k_ref, vt_ref):
    tm = x_ref.shape[1]
    shift = mod_ref[:, 0:D_MODEL]
    scale = mod_ref[:, D_MODEL:2 * D_MODEL]
    h = (_rms(x_ref[0]) * gmix_ref[...]) * (1.0 + scale) + shift
    hb = h.astype(BF16)

    cq = _dot(hb, win_ref[:, OFF_CQ:OFF_CKV])
    ckv = _dot(hb, win_ref[:, OFF_CKV:OFF_MISC])
    misc = _dot(hb, win_ref[:, OFF_MISC:D_IN_PAD])
    misc_ref[0] = misc

    cos = cos_ref[0]
    sin = sin_ref[0]
    lane = lax.broadcasted_iota(jnp.int32, (tm, LANES), 1)
    first_half = (lane & (QK_ROPE // 2)) == 0
    low_half = lane < QK_ROPE
    zero = jnp.zeros((tm, LANES), F32)

    def rope(r):
        swapped = jnp.where(first_half, pltpu.roll(r, LANES - QK_ROPE // 2, 1), pltpu.roll(r, QK_ROPE // 2, 1))
        return r * cos + swapped * sin

    q = _dot((_rms(cq) * qn_ref[...]).astype(BF16), wuq_ref[...]) * Q_PRESCALE
    kv = _dot((_rms(ckv) * kvn_ref[...]).astype(BF16), wukv_ref[...])

    k_rope = jnp.where(low_half, pltpu.roll(rope(misc), QK_ROPE, 1), zero).astype(BF16)
    nope_w = MLA_HEADS * QK_NOPE
    for pair in range(MLA_HEADS // 2):
        rq = rope(q[:, nope_w + pair * LANES:nope_w + (pair + 1) * LANES])
        q_ref[0, 2 * pair, :, QK_NOPE:QK_PAD] = jnp.where(low_half, rq, zero).astype(BF16)
        q_ref[0, 2 * pair + 1, :, QK_NOPE:QK_PAD] = jnp.where(low_half, pltpu.roll(rq, QK_ROPE, 1), zero).astype(BF16)
    for hd in range(MLA_HEADS):
        q_ref[0, hd, :, 0:QK_NOPE] = q[:, hd * QK_NOPE:(hd + 1) * QK_NOPE].astype(BF16)
        k_ref[0, hd, :, 0:QK_NOPE] = kv[:, hd * QK_NOPE:(hd + 1) * QK_NOPE].astype(BF16)
        k_ref[0, hd, :, QK_NOPE:QK_PAD] = k_rope
        for sub in range(tm // T_ATT):
            vt_ref[0, hd, sub] = kv[sub * T_ATT:(sub + 1) * T_ATT,
                                    nope_w + hd * V_DIM:nope_w + (hd + 1) * V_DIM].T.astype(BF16)

    z_ref[0] = _dot(hb, win_ref[:, OFF_Z:OFF_XBC]).astype(BF16)
    xbc_ref[0] = _dot(hb, win_ref[:, OFF_XBC:OFF_CQ]).astype(BF16)


def _in_proj(x, mod_l, gmix, win, qn, wuq, kvn, wukv, cos, sin):
    bsz, seq, _ = x.shape
    tm = min(TM_IN, seq)
    n_sub = tm // T_ATT
    tok = lambda w: pl.BlockSpec((1, tm, w), lambda b, s: (b, s, 0))
    head = lambda w: pl.BlockSpec((1, MLA_HEADS, tm, w), lambda b, s: (b, 0, s, 0))
    return pl.pallas_call(
        _in_kernel,
        grid=(bsz, seq // tm),
        in_specs=[tok(D_MODEL),
                  pl.BlockSpec((None, 1, 6 * D_MODEL), lambda b, s: (b, 0, 0)),
                  _const_spec((1, D_MODEL)),
                  _const_spec((D_MODEL, D_IN_PAD)),
                  _const_spec((1, Q_RANK)),
                  _const_spec((Q_RANK, MLA_HEADS * (QK_NOPE + QK_ROPE))),
                  _const_spec((1, KV_RANK)),
                  _const_spec((KV_RANK, MLA_HEADS * (QK_NOPE + V_DIM))),
                  tok(LANES), tok(LANES)],
        out_specs=[tok(D_SSD), tok(D_XBC), tok(LANES), head(QK_PAD), head(QK_PAD),
                   pl.BlockSpec((1, MLA_HEADS, n_sub, V_DIM, T_ATT), lambda b, s: (b, 0, s, 0, 0))],
        out_shape=[jax.ShapeDtypeStruct((bsz, seq, D_SSD), BF16),
                   jax.ShapeDtypeStruct((bsz, seq, D_XBC), BF16),
                   jax.ShapeDtypeStruct((bsz, seq, LANES), F32),
                   jax.ShapeDtypeStruct((bsz, MLA_HEADS, seq, QK_PAD), BF16),
                   jax.ShapeDtypeStruct((bsz, MLA_HEADS, seq, QK_PAD), BF16),
                   jax.ShapeDtypeStruct((bsz, MLA_HEADS, seq // T_ATT, V_DIM, T_ATT), BF16)],
        compiler_params=_params(2),
        name="in_proj",
    )(x, mod_l, gmix, win, qn, wuq, kvn, wukv, cos, sin)


def _ssd_kernel(z_ref, xbc_ref, misc_ref, shift_ref, cw_ref, cb_ref, dtb_ref, alog_ref, dskip_ref, norm_ref,
                expand_ref, y_ref, ext_scr, act_scr, state_scr):
    t = z_ref.shape[1]
    n_chunks = t // CHUNK

    @pl.when(pl.program_id(1) == 0)
    def _():
        ext_scr[0:HALO_B, :] = jnp.zeros((HALO_B, D_XBC), BF16)
        state_scr[...] = jnp.zeros(state_scr.shape, F32)

    ext_scr[HALO_B:HALO_B + t, :] = xbc_ref[0]

    lane_row = lax.broadcasted_iota(jnp.int32, (1, LANES), 1)
    a_head = jnp.where(lane_row < SSD_HEADS, -jnp.exp(alog_ref[...]) * LOG2E, 0.0)

    row_i = lax.broadcasted_iota(jnp.int32, (CHUNK, CHUNK), 0)
    col_i = lax.broadcasted_iota(jnp.int32, (CHUNK, CHUNK), 1)
    causal = row_i >= col_i
    tril = jnp.where(causal, 1.0, 0.0).astype(BF16)
    lane_c = lax.broadcasted_iota(jnp.int32, (CHUNK, LANES), 1)
    left = lane_c < SSD_HEAD_DIM

    def chunk_body(c, carry):
        r0 = pl.multiple_of(c * CHUNK, CHUNK)
        rows = pl.ds(r0, CHUNK)

        dt_in = misc_ref[0, rows, :] + dtb_ref[...]
        dt = jnp.maximum(dt_in, 0.0) + jnp.log1p(jnp.exp(-jnp.abs(dt_in)))
        a = dt * a_head
        a_hi = a.astype(BF16)
        a_lo = (a - a_hi.astype(F32)).astype(BF16)
        a_cs = _dot(tril, a_hi) + _dot(tril, a_lo)
        a_cs_t = a_cs.T
        dt_t = dt.T
        last = a_cs[CHUNK - 1:CHUNK, :]
        decay_end_dt = (jnp.exp2(last - a_cs) * dt).astype(BF16)
        exp_a = jnp.exp2(a_cs).astype(BF16)

        u2 = ext_scr[pl.ds(r0, HALO_B + CHUNK), :]
        delayed = _dot(shift_ref[...], u2)
        conv = cb_ref[...] + cw_ref[SSD_CONV - 1:SSD_CONV, :] * u2[HALO_B:HALO_B + CHUNK, :].astype(F32)
        for tap in range(SSD_CONV - 1):
            conv = conv + cw_ref[tap:tap + 1, :] * delayed[tap * CHUNK:(tap + 1) * CHUNK, :]
        act = act_scr
        act[...] = _silu(conv)

        heads_per_group = SSD_HEADS // SSD_GROUPS
        for g in range(SSD_GROUPS):
            gs = slice(g * GROUP_W, (g + 1) * GROUP_W)
            xs = act[:, gs]
            xs_b = xs.astype(BF16)
            x_w = (xs * _dot(decay_end_dt, expand_ref[:, gs])).astype(BF16)
            ea_exp = _dot(exp_a, expand_ref[:, gs])
            bm = act[:, D_SSD + g * SSD_STATE:D_SSD + (g + 1) * SSD_STATE]
            cm = act[:, D_SSD + (SSD_GROUPS + g) * SSD_STATE:D_SSD + (SSD_GROUPS + g + 1) * SSD_STATE]
            cm_b = cm.astype(BF16)
            cb = _dot_nt(cm_b, bm.astype(BF16))
            prev = state_scr[g]
            y_off = _dot(cm_b, prev.astype(BF16)) * ea_exp
            state_scr[g] = prev * ea_exp[CHUNK - 1:CHUNK, :] + _dot(bm.T.astype(BF16), x_w)
            y_pairs = []
            for pair in range(heads_per_group // 2):
                k0 = g * heads_per_group + 2 * pair
                ms = []
                for k in (k0, k0 + 1):
                    seg = a_cs[:, k:k + 1] - a_cs_t[k:k + 1, :]
                    decay = jnp.exp2(jnp.where(causal, seg, -jnp.inf))
                    ms.append((cb * decay * dt_t[k:k + 1, :]).astype(BF16))
                xp = xs_b[:, 2 * pair * SSD_HEAD_DIM:2 * pair * SSD_HEAD_DIM + LANES]
                zero = jnp.zeros_like(xp)
                block_diag = jnp.concatenate([jnp.where(left, xp, zero), jnp.where(left, zero, xp)], axis=0)
                y_pairs.append(_dot(jnp.concatenate(ms, axis=1), block_diag))
            y_diag = jnp.concatenate(y_pairs, axis=1)

            y = y_diag + y_off + xs * dskip_ref[:, gs]
            zc = z_ref[0, rows, gs].astype(F32)
            y = y * _silu(zc)
            y_ref[0, rows, gs] = (_rms(y) * norm_ref[:, gs]).astype(BF16)
        return carry

    lax.fori_loop(0, n_chunks, chunk_body, 0)
    ext_scr[0:HALO_B, :] = ext_scr[t:t + HALO_B, :]


def _conv_shift_matrix():
    m = np.zeros(((SSD_CONV - 1) * CHUNK, HALO_B + CHUNK), np.float32)
    for tap in range(SSD_CONV - 1):
        delay = SSD_CONV - 1 - tap
        m[tap * CHUNK + np.arange(CHUNK), HALO_B + np.arange(CHUNK) - delay] = 1.0
    return jnp.asarray(m, BF16)


def _ssd_mixer(z, xbc, misc, cw, cb, dtb, alog, dskip, norm, expand):
    bsz, seq, _ = z.shape
    t = min(T_SSD, seq)
    tok = lambda w: pl.BlockSpec((1, t, w), lambda b, s: (b, s, 0))
    return pl.pallas_call(
        _ssd_kernel,
        grid=(bsz, seq // t),
        in_specs=[tok(D_SSD), tok(D_XBC), tok(LANES),
                  _const_spec(((SSD_CONV - 1) * CHUNK, HALO_B + CHUNK)),
                  _const_spec((SSD_CONV, D_XBC)), _const_spec((1, D_XBC)),
                  _const_spec((1, LANES)), _const_spec((1, LANES)),
                  _const_spec((1, D_SSD)), _const_spec((1, D_SSD)),
                  _const_spec((LANES, D_SSD))],
        out_specs=tok(D_SSD),
        out_shape=jax.ShapeDtypeStruct((bsz, seq, D_SSD), BF16),
        scratch_shapes=[pltpu.VMEM((t + HALO_B, D_XBC), BF16),
                        pltpu.VMEM((CHUNK, D_XBC), F32),
                        pltpu.VMEM((SSD_GROUPS, SSD_STATE, GROUP_W), F32)],
        compiler_params=_params(2),
        name="ssd_mixer",
    )(z, xbc, misc, _conv_shift_matrix(), cw, cb, dtb, alog, dskip, norm, expand)


def _attn_kernel(q_ref, k_ref, vt_ref, o_ref, s_scr, m_scr, l_scr, acc_scr):
    tq = q_ref.shape[2]
    qi = pl.program_id(1)
    keep = lax.broadcasted_iota(jnp.int32, (tq, tq), 0) <= lax.broadcasted_iota(jnp.int32, (tq, tq), 1)

    def qk(c, hd):
        rows = pl.ds(pl.multiple_of(c * tq, tq), tq)
        s_scr[hd % S_SLOTS] = _dot_nt(k_ref[0, hd, rows, :], q_ref[0, hd])

    def softmax_pv(c, hd, masked):
        s = s_scr[hd % S_SLOTS]
        if masked:
            s = jnp.where(keep, s, -jnp.inf)
        m_prev = m_scr[hd]
        m_new = jnp.maximum(m_prev, jnp.max(s, axis=0, keepdims=True))
        alpha = jnp.exp2(m_prev - m_new)
        p = jnp.exp2(s - m_new)
        m_scr[hd] = m_new
        l_scr[hd] = alpha * l_scr[hd] + jnp.sum(p, axis=0, keepdims=True)
        acc_scr[hd] = alpha * acc_scr[hd] + _dot(vt_ref[0, hd, c], p.astype(BF16))

    def chunk(c, last):
        for hd in range(MLA_HEADS):
            ahead = hd + S_AHEAD
            if ahead < MLA_HEADS:
                qk(c, ahead)
            elif not last:
                qk(c + 1, ahead - MLA_HEADS)
            softmax_pv(c, hd, last)
            if last:
                o_ref[0, :, hd * V_DIM:(hd + 1) * V_DIM] = (acc_scr[hd] / l_scr[hd]).T.astype(BF16)

    m_scr[...] = jnp.full(m_scr.shape, -jnp.inf, F32)
    l_scr[...] = jnp.zeros(l_scr.shape, F32)
    acc_scr[...] = jnp.zeros(acc_scr.shape, F32)

    for hd in range(S_AHEAD):
        qk(0, hd)

    def body(c, carry):
        chunk(c, False)
        return carry

    lax.fori_loop(0, qi, body, 0)
    chunk(qi, True)


def _attention(q, k, vt):
    bsz, _, seq, _ = q.shape
    tq = vt.shape[-1]
    return pl.pallas_call(
        _attn_kernel,
        grid=(bsz, seq // tq),
        in_specs=[pl.BlockSpec((1, MLA_HEADS, tq, QK_PAD), lambda b, i: (b, 0, i, 0)),
                  pl.BlockSpec((1, MLA_HEADS, seq, QK_PAD), lambda b, i: (b, 0, 0, 0)),
                  pl.BlockSpec((1, MLA_HEADS, seq // tq, V_DIM, tq), lambda b, i: (b, 0, 0, 0, 0))],
        out_specs=pl.BlockSpec((1, tq, D_ATT), lambda b, i: (b, i, 0)),
        out_shape=jax.ShapeDtypeStruct((bsz, seq, D_ATT), BF16),
        scratch_shapes=[pltpu.VMEM((S_SLOTS, tq, tq), F32),
                        pltpu.VMEM((MLA_HEADS, 1, tq), F32),
                        pltpu.VMEM((MLA_HEADS, 1, tq), F32),
                        pltpu.VMEM((MLA_HEADS, V_DIM, tq), F32)],
        compiler_params=_params(2),
        name="mla_attention",
    )(q, k, vt)


def _out_ffn_kernel(final, y_ref, o_ref, x_ref, mod_ref, anorm_ref, wout_ref, gmlp_ref, wup_ref, cw_ref, cb_ref,
                    wdown_ref, fnorm_ref, out_ref, u_scr, act_scr):
    tm = x_ref.shape[1]

    @pl.when(pl.program_id(1) == 0)
    def _():
        u_scr[0:HALO, :] = jnp.zeros((HALO, 2 * D_FF), F32)

    gate1 = mod_ref[:, 2 * D_MODEL:3 * D_MODEL]
    shift2 = mod_ref[:, 3 * D_MODEL:4 * D_MODEL]
    scale2 = mod_ref[:, 4 * D_MODEL:5 * D_MODEL]
    gate2 = mod_ref[:, 5 * D_MODEL:6 * D_MODEL]

    o_n = (_rms(o_ref[0].astype(F32)) * anorm_ref[...]).astype(BF16)
    y = _dot(y_ref[0], wout_ref[0:D_SSD, :]) + _dot(o_n, wout_ref[D_SSD:D_SSD + D_ATT, :])
    x1 = x_ref[0] + gate1 * y
    hb = ((_rms(x1) * gmlp_ref[...]) * (1.0 + scale2) + shift2).astype(BF16)

    base = HALO - (FF_CONV - 1)
    for c in range(D_FF // FF_CHUNK):
        conv = []
        for off in (c * FF_CHUNK, D_FF + c * FF_CHUNK):
            cols = slice(off, off + FF_CHUNK)
            u_scr[HALO:HALO + tm, cols] = _dot(hb, wup_ref[:, cols])
            acc = cb_ref[:, cols] + cw_ref[0:1, cols] * u_scr[base:base + tm, cols]
            for tap in range(1, FF_CONV):
                acc = acc + cw_ref[tap:tap + 1, cols] * u_scr[base + tap:base + tap + tm, cols]
            conv.append(acc)
        act_scr[:, c * FF_CHUNK:(c + 1) * FF_CHUNK] = (_silu(conv[0]) * conv[1]).astype(BF16)
    u_scr[0:HALO, :] = u_scr[tm:tm + HALO, :]

    x2 = x1 + gate2 * _dot(act_scr[...], wdown_ref[...])
    if final:
        x2 = _rms(x2) * fnorm_ref[...]
    out_ref[0] = x2


def _out_ffn(final, y_ssd, o, x, mod_l, anorm, wout, gmlp, wup, cw, cb, wdown, fnorm):
    bsz, seq, _ = x.shape
    tm = min(TM_FFN, seq)
    tok = lambda w: pl.BlockSpec((1, tm, w), lambda b, s: (b, s, 0))
    return pl.pallas_call(
        functools.partial(_out_ffn_kernel, final),
        grid=(bsz, seq // tm),
        in_specs=[tok(D_SSD), tok(D_ATT), tok(D_MODEL),
                  pl.BlockSpec((None, 1, 6 * D_MODEL), lambda b, s: (b, 0, 0)),
                  _const_spec((1, D_ATT)),
                  _const_spec((D_SSD + D_ATT, D_MODEL)),
                  _const_spec((1, D_MODEL)),
                  _const_spec((D_MODEL, 2 * D_FF)),
                  _const_spec((FF_CONV, 2 * D_FF)),
                  _const_spec((1, 2 * D_FF)),
                  _const_spec((D_FF, D_MODEL)),
                  _const_spec((1, D_MODEL))],
        out_specs=tok(D_MODEL),
        out_shape=jax.ShapeDtypeStruct((bsz, seq, D_MODEL), F32),
        scratch_shapes=[pltpu.VMEM((tm + HALO, 2 * D_FF), F32),
                        pltpu.VMEM((tm, D_FF), BF16)],
        compiler_params=_params(2),
        name="out_ffn_final" if final else "out_ffn",
    )(y_ssd, o, x, mod_l, anorm, wout, gmlp, wup, cw, cb, wdown, fnorm)


def _pad_lanes(v):
    return jnp.pad(v.astype(F32), (0, LANES - v.shape[0]))[None, :]


def kernel(x, c, positions, w_ada, b_ada, norm_mix, w_in, conv_w, conv_b, dt_bias, a_log, d_skip, ssd_norm,
           q_norm, w_uq, kv_norm, w_ukv, attn_norm, w_out, norm_mlp, w_up, conv_ff_w, conv_ff_b, w_down,
           final_norm):
    depth = w_in.shape[0]
    bsz = x.shape[0]
    mod = _modulation(c, w_ada, b_ada).reshape(depth, bsz, 1, 6 * D_MODEL)
    cos, sin = _rope_tables(positions)

    expand = jnp.asarray(np.kron(np.eye(LANES, SSD_HEADS, dtype=np.float32),
                                 np.ones((1, SSD_HEAD_DIM), np.float32)), BF16)

    s_z, s_xbc, s_dt = D_SSD, D_SSD + D_XBC, D_SSD + D_XBC + SSD_HEADS
    s_cq, s_ckv = s_dt + Q_RANK, s_dt + Q_RANK + KV_RANK

    for l in range(depth):
        wi = w_in[l].astype(BF16)
        win = jnp.concatenate(
            [wi[:, :s_z], wi[:, s_z:s_xbc], wi[:, s_dt:s_cq], wi[:, s_cq:s_ckv], wi[:, s_xbc:s_dt],
             jnp.zeros((D_MODEL, LANES - SSD_HEADS - QK_ROPE), BF16), wi[:, s_ckv:]], axis=1)
        uq = w_uq[l].astype(BF16).reshape(Q_RANK, MLA_HEADS, QK_NOPE + QK_ROPE)
        wuq = jnp.concatenate([uq[:, :, :QK_NOPE].reshape(Q_RANK, -1), uq[:, :, QK_NOPE:].reshape(Q_RANK, -1)],
                              axis=1)
        ukv = w_ukv[l].astype(BF16).reshape(KV_RANK, MLA_HEADS, QK_NOPE + V_DIM)
        wukv = jnp.concatenate([ukv[:, :, :QK_NOPE].reshape(KV_RANK, -1), ukv[:, :, QK_NOPE:].reshape(KV_RANK, -1)],
                               axis=1)

        z, xbc, misc, q, k, vt = _in_proj(x, mod[l], norm_mix[l][None, :], win, q_norm[l][None, :], wuq,
                                          kv_norm[l][None, :], wukv, cos, sin)
        y_ssd = _ssd_mixer(z, xbc, misc, conv_w[l], conv_b[l][None, :], _pad_lanes(dt_bias[l]), _pad_lanes(a_log[l]),
                           jnp.repeat(d_skip[l], SSD_HEAD_DIM)[None, :], ssd_norm[l][None, :], expand)
        o = _attention(q, k, vt)
        x = _out_ffn(l == depth - 1, y_ssd, o, x, mod[l], attn_norm[l][None, :], w_out[l].astype(BF16),
                     norm_mlp[l][None, :], w_up[l].astype(BF16), conv_ff_w[l], conv_ff_b[l][None, :],
                     w_down[l].astype(BF16), final_norm[None, :])
    return x
```

```python
import functools

import jax
import jax.numpy as jnp
import numpy as np
from jax import lax
from jax.experimental import pallas as pl
from jax.experimental.pallas import tpu as pltpu

F32 = jnp.float32
BF16 = jnp.bfloat16

LANES = 128
D_MODEL = 1024
D_SSD = 1024
SSD_HEADS = 16
SSD_HEAD_DIM = 64
SSD_GROUPS = 2
SSD_STATE = 128
SSD_CONV = 4
CHUNK = 128
D_XBC = D_SSD + 2 * SSD_GROUPS * SSD_STATE
GROUP_W = D_SSD // SSD_GROUPS
MLA_HEADS = 8
QK_NOPE = 128
QK_ROPE = 64
QK_PAD = 256
V_DIM = 128
D_ATT = MLA_HEADS * V_DIM
Q_RANK = 384
KV_RANK = 256
ROPE_BASE = 10000.0
D_FF = 2816
FF_CONV = 3
FF_CHUNK = 256
EPS = 1e-6
HALO = 8
HALO_B = 16

OFF_Z = 0
OFF_XBC = OFF_Z + D_SSD
OFF_CQ = OFF_XBC + D_XBC
OFF_CKV = OFF_CQ + Q_RANK
OFF_MISC = OFF_CKV + KV_RANK
D_IN_PAD = OFF_MISC + LANES

VMEM_LIMIT = 56 * 1024 * 1024

T_ATT = 512
TM_IN = 1024
T_SSD = 512
TM_FFN = 512

LOG2E = float(np.log2(np.e))
Q_PRESCALE = float((QK_NOPE + QK_ROPE) ** -0.5) * LOG2E

S_AHEAD = 2
S_SLOTS = 4
assert S_SLOTS > S_AHEAD and MLA_HEADS % S_SLOTS == 0


def _dot(a, b):
    return jnp.dot(a, b, preferred_element_type=F32)


def _dot_nt(a, b):
    return lax.dot_general(a, b, (((1,), (1,)), ((), ())), preferred_element_type=F32)


def _silu(v):
    h = 0.5 * v
    return h + h * jnp.tanh(h)


def _rms(v):
    return v * lax.rsqrt(jnp.mean(v * v, axis=-1, keepdims=True) + EPS)


def _const_spec(shape):
    nd = len(shape)
    return pl.BlockSpec(shape, lambda *_: (0,) * nd, pipeline_mode=pl.Buffered(1))


def _params(n_axes):
    return pltpu.CompilerParams(dimension_semantics=("arbitrary",) * n_axes,
                                vmem_limit_bytes=VMEM_LIMIT)


def _mod_kernel(c_ref, w_ref, b_ref, o_ref):
    c = c_ref[...]
    c_act = _silu(c).astype(BF16)
    o_ref[0] = _dot(c_act, w_ref[0].astype(BF16)) + b_ref[0]


def _modulation(c, w_ada, b_ada):
    depth, _, n_out = w_ada.shape
    bsz = c.shape[0]
    nblk = n_out // D_MODEL
    return pl.pallas_call(
        _mod_kernel,
        grid=(depth, nblk),
        in_specs=[pl.BlockSpec((bsz, D_MODEL), lambda l, j: (0, 0)),
                  pl.BlockSpec((1, D_MODEL, D_MODEL), lambda l, j: (l, 0, j)),
                  pl.BlockSpec((1, 1, D_MODEL), lambda l, j: (l, 0, j))],
        out_specs=pl.BlockSpec((1, bsz, D_MODEL), lambda l, j: (l, 0, j)),
        out_shape=jax.ShapeDtypeStruct((depth, bsz, n_out), F32),
        compiler_params=_params(2),
        name="adaln_mod",
    )(c, w_ada, b_ada.reshape(depth, 1, n_out))


def _rope_kernel(pos_ref, freq_ref, sign_ref, cos_ref, sin_ref):
    ang = pos_ref[0] * freq_ref[...]
    cos_ref[0] = jnp.cos(ang)
    sin_ref[0] = jnp.sin(ang) * sign_ref[...]


def _rope_tables(positions):
    bsz, seq = positions.shape
    half = QK_ROPE // 2
    inv_freq = 1.0 / (ROPE_BASE ** (np.arange(0, QK_ROPE, 2, dtype=np.float32) / QK_ROPE))
    freq = jnp.asarray(np.tile(inv_freq, LANES // half)[None, :], F32)
    sign = jnp.asarray(np.tile(np.concatenate([-np.ones(half), np.ones(half)]), LANES // QK_ROPE)[None, :], F32)
    pos = jnp.broadcast_to(positions.astype(F32)[..., None], (bsz, seq, LANES))
    tile = min(seq, 1024)
    spec = pl.BlockSpec((1, tile, LANES), lambda b, s: (b, s, 0))
    row = pl.BlockSpec((1, LANES), lambda b, s: (0, 0))
    return pl.pallas_call(
        _rope_kernel,
        grid=(bsz, seq // tile),
        in_specs=[spec, row, row],
        out_specs=[spec, spec],
        out_shape=[jax.ShapeDtypeStruct((bsz, seq, LANES), F32)] * 2,
        compiler_params=_params(2),
        name="rope_tables",
    )(pos, freq, sign)


def _in_kernel(x_ref, mod_ref, gmix_ref, win_ref, qn_ref, wuq_ref, kvn_ref, wukv_ref, cos_ref, sin_ref,
               z_ref, xbc_ref, misc_ref, q_ref, k_ref, vt_ref):
    tm = x_ref.shape[1]
    shift = mod_ref[:, 0:D_MODEL]
    scale = mod_ref[:, D_MODEL:2 * D_MODEL]
    h = (_rms(x_ref[0]) * gmix_ref[...]) * (1.0 + scale) + shift
    hb = h.astype(BF16)

    cq = _dot(hb, win_ref[:, OFF_CQ:OFF_CKV])
    ckv = _dot(hb, win_ref[:, OFF_CKV:OFF_MISC])
    misc = _dot(hb, win_ref[:, OFF_MISC:D_IN_PAD])
    misc_ref[0] = misc

    cos = cos_ref[0]
    sin = sin_ref[0]
    lane = lax.broadcasted_iota(jnp.int32, (tm, LANES), 1)
    first_half = (lane & (QK_ROPE // 2)) == 0
    low_half = lane < QK_ROPE
    zero = jnp.zeros((tm, LANES), F32)

    def rope(r):
        swapped = jnp.where(first_half, pltpu.roll(r, LANES - QK_ROPE // 2, 1), pltpu.roll(r, QK_ROPE // 2, 1))
        return r * cos + swapped * sin

    q = _dot((_rms(cq) * qn_ref[...]).astype(BF16), wuq_ref[...]) * Q_PRESCALE
    kv = _dot((_rms(ckv) * kvn_ref[...]).astype(BF16), wukv_ref[...])

    k_rope = jnp.where(low_half, pltpu.roll(rope(misc), QK_ROPE, 1), zero).astype(BF16)
    nope_w = MLA_HEADS * QK_NOPE
    for pair in range(MLA_HEADS // 2):
        rq = rope(q[:, nope_w + pair * LANES:nope_w + (pair + 1) * LANES])
        q_ref[0, 2 * pair, :, QK_NOPE:QK_PAD] = jnp.where(low_half, rq, zero).astype(BF16)
        q_ref[0, 2 * pair + 1, :, QK_NOPE:QK_PAD] = jnp.where(low_half, pltpu.roll(rq, QK_ROPE, 1), zero).astype(BF16)
    for hd in range(MLA_HEADS):
        q_ref[0, hd, :, 0:QK_NOPE] = q[:, hd * QK_NOPE:(hd + 1) * QK_NOPE].astype(BF16)
        k_ref[0, hd, :, 0:QK_NOPE] = kv[:, hd * QK_NOPE:(hd + 1) * QK_NOPE].astype(BF16)
        k_ref[0, hd, :, QK_NOPE:QK_PAD] = k_rope
        for sub in range(tm // T_ATT):
            vt_ref[0, hd, sub] = kv[sub * T_ATT:(sub + 1) * T_ATT,
                                    nope_w + hd * V_DIM:nope_w + (hd + 1) * V_DIM].T.astype(BF16)

    z_ref[0] = _dot(hb, win_ref[:, OFF_Z:OFF_XBC]).astype(BF16)
    xbc_ref[0] = _dot(hb, win_ref[:, OFF_XBC:OFF_CQ]).astype(BF16)


def _in_proj(x, mod_l, gmix, win, qn, wuq, kvn, wukv, cos, sin):
    bsz, seq, _ = x.shape
    tm = min(TM_IN, seq)
    n_sub = tm // T_ATT
    tok = lambda w: pl.BlockSpec((1, tm, w), lambda b, s: (b, s, 0))
    head = lambda w: pl.BlockSpec((1, MLA_HEADS, tm, w), lambda b, s: (b, 0, s, 0))
    return pl.pallas_call(
        _in_kernel,
        grid=(bsz, seq // tm),
        in_specs=[tok(D_MODEL),
                  pl.BlockSpec((None, 1, 6 * D_MODEL), lambda b, s: (b, 0, 0)),
                  _const_spec((1, D_MODEL)),
                  _const_spec((D_MODEL, D_IN_PAD)),
                  _const_spec((1, Q_RANK)),
                  _const_spec((Q_RANK, MLA_HEADS * (QK_NOPE + QK_ROPE))),
                  _const_spec((1, KV_RANK)),
                  _const_spec((KV_RANK, MLA_HEADS * (QK_NOPE + V_DIM))),
                  tok(LANES), tok(LANES)],
        out_specs=[tok(D_SSD), tok(D_XBC), tok(LANES), head(QK_PAD), head(QK_PAD),
                   pl.BlockSpec((1, MLA_HEADS, n_sub, V_DIM, T_ATT), lambda b, s: (b, 0, s, 0, 0))],
        out_shape=[jax.ShapeDtypeStruct((bsz, seq, D_SSD), BF16),
                   jax.ShapeDtypeStruct((bsz, seq, D_XBC), BF16),
                   jax.ShapeDtypeStruct((bsz, seq, LANES), F32),
                   jax.ShapeDtypeStruct((bsz, MLA_HEADS, seq, QK_PAD), BF16),
                   jax.ShapeDtypeStruct((bsz, MLA_HEADS, seq, QK_PAD), BF16),
                   jax.ShapeDtypeStruct((bsz, MLA_HEADS, seq // T_ATT, V_DIM, T_ATT), BF16)],
        compiler_params=_params(2),
        name="in_proj",
    )(x, mod_l, gmix, win, qn, wuq, kvn, wukv, cos, sin)


def _ssd_kernel(z_ref, xbc_ref, misc_ref, shift_ref, cw_ref, cb_ref, dtb_ref, alog_ref, dskip_ref, norm_ref,
                expand_ref, y_ref, ext_scr, act_scr, state_scr):
    t = z_ref.shape[1]
    n_chunks = t // CHUNK

    @pl.when(pl.program_id(1) == 0)
    def _():
        ext_scr[0:HALO_B, :] = jnp.zeros((HALO_B, D_XBC), BF16)
        state_scr[...] = jnp.zeros(state_scr.shape, F32)

    ext_scr[HALO_B:HALO_B + t, :] = xbc_ref[0]

    lane_row = lax.broadcasted_iota(jnp.int32, (1, LANES), 1)
    a_head = jnp.where(lane_row < SSD_HEADS, -jnp.exp(alog_ref[...]) * LOG2E, 0.0)

    row_i = lax.broadcasted_iota(jnp.int32, (CHUNK, CHUNK), 0)
    col_i = lax.broadcasted_iota(jnp.int32, (CHUNK, CHUNK), 1)
    causal = row_i >= col_i
    tril = jnp.where(causal, 1.0, 0.0).astype(BF16)
    lane_c = lax.broadcasted_iota(jnp.int32, (CHUNK, LANES), 1)
    left = lane_c < SSD_HEAD_DIM

    def chunk_body(c, carry):
        r0 = pl.multiple_of(c * CHUNK, CHUNK)
        rows = pl.ds(r0, CHUNK)

        dt_in = misc_ref[0, rows, :] + dtb_ref[...]
        dt = jnp.maximum(dt_in, 0.0) + jnp.log1p(jnp.exp(-jnp.abs(dt_in)))
        a = dt * a_head
        a_hi = a.astype(BF16)
        a_lo = (a - a_hi.astype(F32)).astype(BF16)
        a_cs = _dot(tril, a_hi) + _dot(tril, a_lo)
        a_cs_t = a_cs.T
        dt_t = dt.T
        last = a_cs[CHUNK - 1:CHUNK, :]
        decay_end_dt = (jnp.exp2(last - a_cs) * dt).astype(BF16)
        exp_a = jnp.exp2(a_cs).astype(BF16)

        u2 = ext_scr[pl.ds(r0, HALO_B + CHUNK), :]
        delayed = _dot(shift_ref[...], u2)
        conv = cb_ref[...] + cw_ref[SSD_CONV - 1:SSD_CONV, :] * u2[HALO_B:HALO_B + CHUNK, :].astype(F32)
        for tap in range(SSD_CONV - 1):
            conv = conv + cw_ref[tap:tap + 1, :] * delayed[tap * CHUNK:(tap + 1) * CHUNK, :]
        act = act_scr
        act[...] = _silu(conv)

        heads_per_group = SSD_HEADS // SSD_GROUPS
        for g in range(SSD_GROUPS):
            gs = slice(g * GROUP_W, (g + 1) * GROUP_W)
            xs = act[:, gs]
            xs_b = xs.astype(BF16)
            x_w = (xs * _dot(decay_end_dt, expand_ref[:, gs])).astype(BF16)
            ea_exp = _dot(exp_a, expand_ref[:, gs])
            bm = act[:, D_SSD + g * SSD_STATE:D_SSD + (g + 1) * SSD_STATE]
            cm = act[:, D_SSD + (SSD_GROUPS + g) * SSD_STATE:D_SSD + (SSD_GROUPS + g + 1) * SSD_STATE]
            cm_b = cm.astype(BF16)
            cb = _dot_nt(cm_b, bm.astype(BF16))
            prev = state_scr[g]
            y_off = _dot(cm_b, prev.astype(BF16)) * ea_exp
            state_scr[g] = prev * ea_exp[CHUNK - 1:CHUNK, :] + _dot(bm.T.astype(BF16), x_w)
            y_pairs = []
            for pair in range(heads_per_group // 2):
                k0 = g * heads_per_group + 2 * pair
                ms = []
                for k in (k0, k0 + 1):
                    seg = a_cs[:, k:k + 1] - a_cs_t[k:k + 1, :]
                    decay = jnp.exp2(jnp.where(causal, seg, -jnp.inf))
                    ms.append((cb * decay * dt_t[k:k + 1, :]).astype(BF16))
                xp = xs_b[:, 2 * pair * SSD_HEAD_DIM:2 * pair * SSD_HEAD_DIM + LANES]
                zero = jnp.zeros_like(xp)
                block_diag = jnp.concatenate([jnp.where(left, xp, zero), jnp.where(left, zero, xp)], axis=0)
                y_pairs.append(_dot(jnp.concatenate(ms, axis=1), block_diag))
            y_diag = jnp.concatenate(y_pairs, axis=1)

            y = y_diag + y_off + xs * dskip_ref[:, gs]
            zc = z_ref[0, rows, gs].astype(F32)
            y = y * _silu(zc)
            y_ref[0, rows, gs] = (_rms(y) * norm_ref[:, gs]).astype(BF16)
        return carry

    lax.fori_loop(0, n_chunks, chunk_body, 0, unroll=2)
    ext_scr[0:HALO_B, :] = ext_scr[t:t + HALO_B, :]


def _conv_shift_matrix():
    m = np.zeros(((SSD_CONV - 1) * CHUNK, HALO_B + CHUNK), np.float32)
    for tap in range(SSD_CONV - 1):
        delay = SSD_CONV - 1 - tap
        m[tap * CHUNK + np.arange(CHUNK), HALO_B + np.arange(CHUNK) - delay] = 1.0
    return jnp.asarray(m, BF16)


def _ssd_mixer(z, xbc, misc, cw, cb, dtb, alog, dskip, norm, expand):
    bsz, seq, _ = z.shape
    t = min(T_SSD, seq)
    tok = lambda w: pl.BlockSpec((1, t, w), lambda b, s: (b, s, 0))
    return pl.pallas_call(
        _ssd_kernel,
        grid=(bsz, seq // t),
        in_specs=[tok(D_SSD), tok(D_XBC), tok(LANES),
                  _const_spec(((SSD_CONV - 1) * CHUNK, HALO_B + CHUNK)),
                  _const_spec((SSD_CONV, D_XBC)), _const_spec((1, D_XBC)),
                  _const_spec((1, LANES)), _const_spec((1, LANES)),
                  _const_spec((1, D_SSD)), _const_spec((1, D_SSD)),
                  _const_spec((LANES, D_SSD))],
        out_specs=tok(D_SSD),
        out_shape=jax.ShapeDtypeStruct((bsz, seq, D_SSD), BF16),
        scratch_shapes=[pltpu.VMEM((t + HALO_B, D_XBC), BF16),
                        pltpu.VMEM((CHUNK, D_XBC), F32),
                        pltpu.VMEM((SSD_GROUPS, SSD_STATE, GROUP_W), F32)],
        compiler_params=_params(2),
        name="ssd_mixer",
    )(z, xbc, misc, _conv_shift_matrix(), cw, cb, dtb, alog, dskip, norm, expand)


def _attn_kernel(q_ref, k_ref, vt_ref, o_ref, s_scr, m_scr, l_scr, acc_scr):
    tq = q_ref.shape[2]
    qi = pl.program_id(1)
    keep = lax.broadcasted_iota(jnp.int32, (tq, tq), 0) <= lax.broadcasted_iota(jnp.int32, (tq, tq), 1)

    def qk(c, hd):
        rows = pl.ds(pl.multiple_of(c * tq, tq), tq)
        s_scr[hd % S_SLOTS] = _dot_nt(k_ref[0, hd, rows, :], q_ref[0, hd])

    def softmax_pv(c, hd, masked):
        s = s_scr[hd % S_SLOTS]
        if masked:
            s = jnp.where(keep, s, -jnp.inf)
        m_prev = m_scr[hd]
        m_new = jnp.maximum(m_prev, jnp.max(s, axis=0, keepdims=True))
        alpha = jnp.exp2(m_prev - m_new)
        p = jnp.exp2(s - m_new)
        m_scr[hd] = m_new
        l_scr[hd] = alpha * l_scr[hd] + jnp.sum(p, axis=0, keepdims=True)
        acc_scr[hd] = alpha * acc_scr[hd] + _dot(vt_ref[0, hd, c], p.astype(BF16))

    def chunk(c, last):
        for hd in range(MLA_HEADS):
            ahead = hd + S_AHEAD
            if ahead < MLA_HEADS:
                qk(c, ahead)
            elif not last:
                qk(c + 1, ahead - MLA_HEADS)
            softmax_pv(c, hd, last)
            if last:
                o_ref[0, :, hd * V_DIM:(hd + 1) * V_DIM] = (acc_scr[hd] / l_scr[hd]).T.astype(BF16)

    m_scr[...] = jnp.full(m_scr.shape, -jnp.inf, F32)
    l_scr[...] = jnp.zeros(l_scr.shape, F32)
    acc_scr[...] = jnp.zeros(acc_scr.shape, F32)

    for hd in range(S_AHEAD):
        qk(0, hd)

    def body(c, carry):
        chunk(c, False)
        return carry

    lax.fori_loop(0, qi, body, 0)
    chunk(qi, True)


def _attention(q, k, vt):
    bsz, _, seq, _ = q.shape
    tq = vt.shape[-1]
    return pl.pallas_call(
        _attn_kernel,
        grid=(bsz, seq // tq),
        in_specs=[pl.BlockSpec((1, MLA_HEADS, tq, QK_PAD), lambda b, i: (b, 0, i, 0)),
                  pl.BlockSpec((1, MLA_HEADS, seq, QK_PAD), lambda b, i: (b, 0, 0, 0)),
                  pl.BlockSpec((1, MLA_HEADS, seq // tq, V_DIM, tq), lambda b, i: (b, 0, 0, 0, 0))],
        out_specs=pl.BlockSpec((1, tq, D_ATT), lambda b, i: (b, i, 0)),
        out_shape=jax.ShapeDtypeStruct((bsz, seq, D_ATT), BF16),
        scratch_shapes=[pltpu.VMEM((S_SLOTS, tq, tq), F32),
                        pltpu.VMEM((MLA_HEADS, 1, tq), F32),
                        pltpu.VMEM((MLA_HEADS, 1, tq), F32),
                        pltpu.VMEM((MLA_HEADS, V_DIM, tq), F32)],
        compiler_params=_params(2),
        name="mla_attention",
    )(q, k, vt)


def _out_ffn_kernel(final, y_ref, o_ref, x_ref, mod_ref, anorm_ref, wout_ref, gmlp_ref, wup_ref, cw_ref, cb_ref,
                    wdown_ref, fnorm_ref, out_ref, u_scr, act_scr):
    tm = x_ref.shape[1]

    @pl.when(pl.program_id(1) == 0)
    def _():
        u_scr[0:HALO, :] = jnp.zeros((HALO, 2 * D_FF), F32)

    gate1 = mod_ref[:, 2 * D_MODEL:3 * D_MODEL]
    shift2 = mod_ref[:, 3 * D_MODEL:4 * D_MODEL]
    scale2 = mod_ref[:, 4 * D_MODEL:5 * D_MODEL]
    gate2 = mod_ref[:, 5 * D_MODEL:6 * D_MODEL]

    o_n = (_rms(o_ref[0].astype(F32)) * anorm_ref[...]).astype(BF16)
    y = _dot(y_ref[0], wout_ref[0:D_SSD, :]) + _dot(o_n, wout_ref[D_SSD:D_SSD + D_ATT, :])
    x1 = x_ref[0] + gate1 * y
    hb = ((_rms(x1) * gmlp_ref[...]) * (1.0 + scale2) + shift2).astype(BF16)

    base = HALO - (FF_CONV - 1)
    for c in range(D_FF // FF_CHUNK):
        conv = []
        for off in (c * FF_CHUNK, D_FF + c * FF_CHUNK):
            cols = slice(off, off + FF_CHUNK)
            u_scr[HALO:HALO + tm, cols] = _dot(hb, wup_ref[:, cols])
            acc = cb_ref[:, cols] + cw_ref[0:1, cols] * u_scr[base:base + tm, cols]
            for tap in range(1, FF_CONV):
                acc = acc + cw_ref[tap:tap + 1, cols] * u_scr[base + tap:base + tap + tm, cols]
            conv.append(acc)
        act_scr[:, c * FF_CHUNK:(c + 1) * FF_CHUNK] = (_silu(conv[0]) * conv[1]).astype(BF16)
    u_scr[0:HALO, :] = u_scr[tm:tm + HALO, :]

    x2 = x1 + gate2 * _dot(act_scr[...], wdown_ref[...])
    if final:
        x2 = _rms(x2) * fnorm_ref[...]
    out_ref[0] = x2


def _out_ffn(final, y_ssd, o, x, mod_l, anorm, wout, gmlp, wup, cw, cb, wdown, fnorm):
    bsz, seq, _ = x.shape
    tm = min(TM_FFN, seq)
    tok = lambda w: pl.BlockSpec((1, tm, w), lambda b, s: (b, s, 0))
    return pl.pallas_call(
        functools.partial(_out_ffn_kernel, final),
        grid=(bsz, seq // tm),
        in_specs=[tok(D_SSD), tok(D_ATT), tok(D_MODEL),
                  pl.BlockSpec((None, 1, 6 * D_MODEL), lambda b, s: (b, 0, 0)),
                  _const_spec((1, D_ATT)),
                  _const_spec((D_SSD + D_ATT, D_MODEL)),
                  _const_spec((1, D_MODEL)),
                  _const_spec((D_MODEL, 2 * D_FF)),
                  _const_spec((FF_CONV, 2 * D_FF)),
                  _const_spec((1, 2 * D_FF)),
                  _const_spec((D_FF, D_MODEL)),
                  _const_spec((1, D_MODEL))],
        out_specs=tok(D_MODEL),
        out_shape=jax.ShapeDtypeStruct((bsz, seq, D_MODEL), F32),
        scratch_shapes=[pltpu.VMEM((tm + HALO, 2 * D_FF), F32),
                        pltpu.VMEM((tm, D_FF), BF16)],
        compiler_params=_params(2),
        name="out_ffn_final" if final else "out_ffn",
    )(y_ssd, o, x, mod_l, anorm, wout, gmlp, wup, cw, cb, wdown, fnorm)


def _pad_lanes(v):
    return jnp.pad(v.astype(F32), (0, LANES - v.shape[0]))[None, :]


def kernel(x, c, positions, w_ada, b_ada, norm_mix, w_in, conv_w, conv_b, dt_bias, a_log, d_skip, ssd_norm,
           q_norm, w_uq, kv_norm, w_ukv, attn_norm, w_out, norm_mlp, w_up, conv_ff_w, conv_ff_b, w_down,
           final_norm):
    depth = w_in.shape[0]
    bsz = x.shape[0]
    mod = _modulation(c, w_ada, b_ada).reshape(depth, bsz, 1, 6 * D_MODEL)
    cos, sin = _rope_tables(positions)

    expand = jnp.asarray(np.kron(np.eye(LANES, SSD_HEADS, dtype=np.float32),
                                 np.ones((1, SSD_HEAD_DIM), np.float32)), BF16)

    s_z, s_xbc, s_dt = D_SSD, D_SSD + D_XBC, D_SSD + D_XBC + SSD_HEADS
    s_cq, s_ckv = s_dt + Q_RANK, s_dt + Q_RANK + KV_RANK

    for l in range(depth):
        wi = w_in[l].astype(BF16)
        win = jnp.concatenate(
            [wi[:, :s_z], wi[:, s_z:s_xbc], wi[:, s_dt:s_cq], wi[:, s_cq:s_ckv], wi[:, s_xbc:s_dt],
             jnp.zeros((D_MODEL, LANES - SSD_HEADS - QK_ROPE), BF16), wi[:, s_ckv:]], axis=1)
        uq = w_uq[l].astype(BF16).reshape(Q_RANK, MLA_HEADS, QK_NOPE + QK_ROPE)
        wuq = jnp.concatenate([uq[:, :, :QK_NOPE].reshape(Q_RANK, -1), uq[:, :, QK_NOPE:].reshape(Q_RANK, -1)],
                              axis=1)
        ukv = w_ukv[l].astype(BF16).reshape(KV_RANK, MLA_HEADS, QK_NOPE + V_DIM)
        wukv = jnp.concatenate([ukv[:, :, :QK_NOPE].reshape(KV_RANK, -1), ukv[:, :, QK_NOPE:].reshape(KV_RANK, -1)],
                               axis=1)

        z, xbc, misc, q, k, vt = _in_proj(x, mod[l], norm_mix[l][None, :], win, q_norm[l][None, :], wuq,
                                          kv_norm[l][None, :], wukv, cos, sin)
        y_ssd = _ssd_mixer(z, xbc, misc, conv_w[l], conv_b[l][None, :], _pad_lanes(dt_bias[l]), _pad_lanes(a_log[l]),
                           jnp.repeat(d_skip[l], SSD_HEAD_DIM)[None, :], ssd_norm[l][None, :], expand)
        o = _attention(q, k, vt)
        x = _out_ffn(l == depth - 1, y_ssd, o, x, mod[l], attn_norm[l][None, :], w_out[l].astype(BF16),
                     norm_mlp[l][None, :], w_up[l].astype(BF16), conv_ff_w[l], conv_ff_b[l][None, :],
                     w_down[l].astype(BF16), final_norm[None, :])
    return x
```
